```python
import jax
import jax.numpy as jnp
from jax import lax
import numpy as np

D_MODEL = 4096
BATCH = 1
SEQ = 8192
DEPTH = 2

GRID_W = 64
CTX_LEN = 256
N_MIXERS = 2
N_EVEN = (DEPTH + 1) // 2
N_ODD = DEPTH // 2
N_MOD = 6
NORM_EPS = 1e-6
RWKV_HEAD = 64
RWKV_HEADS = D_MODEL // RWKV_HEAD
DECAY_LORA = max(32, int(round(1.8 * D_MODEL ** 0.5 / 32)) * 32)
ICLR_LORA = max(32, int(round(1.8 * D_MODEL ** 0.5 / 32)) * 32)
GATE_LORA = max(32, int(round(0.6 * D_MODEL ** 0.8 / 32)) * 32)
LNX_EPS = 64e-5
N_DIRS = 2
HEAD_DIM = 128
N_Q_HEADS = D_MODEL // HEAD_DIM
N_KV_HEADS = N_Q_HEADS // 4
Q_PER_KV = N_Q_HEADS // N_KV_HEADS
D_ATTN = N_Q_HEADS * HEAD_DIM
D_KV = N_KV_HEADS * HEAD_DIM
Q_BLOCK = 128
ROPE_THETA = 10000.0
ROPE_PAIRS_PER_AXIS = HEAD_DIM // 4
D_FF = 256 * ((int(8 * D_MODEL / 3) + 255) // 256)
N_EXPERTS = 8
TOP_K = 2
D_FF_EXPERT = D_MODEL

kernel_name = 'hybrid_rwkv7_gqa_moe_dit_block'


def rms_norm(x, gain):
    xf = x.astype(jnp.float32)
    y = xf * lax.rsqrt(jnp.mean(xf * xf, axis=-1, keepdims=True) + NORM_EPS)
    return (y * gain.astype(jnp.float32)).astype(x.dtype)


def modulate(x, shift, scale):
    return x * (1.0 + scale) + shift


def ada_params(cond, w, b):
    m = jax.nn.silu(cond) @ w + b
    return jnp.split(m[..., None, :], N_MOD, axis=-1)


def centred_shift_delta(h):
    prev = jnp.pad(h[:, :-1], ((0, 0), (1, 0), (0, 0)))
    nxt = jnp.pad(h[:, 1:], ((0, 0), (0, 1), (0, 0)))
    return 0.5 * (prev + nxt) - h


def rwkv7_scan(r, w, k, v, a, b):
    B, T, H, N = r.shape

    def step(state, inp):
        r_t, w_t, k_t, v_t, a_t, b_t = inp
        sa = jnp.einsum('bhvk,bhk->bhv', state, a_t)
        state = (state * w_t[:, :, None, :] + sa[..., None] * b_t[:, :, None, :]
                 + v_t[..., None] * k_t[:, :, None, :])
        return state, jnp.einsum('bhvk,bhk->bhv', state, r_t)

    xs = tuple(jnp.moveaxis(t, 1, 0) for t in (r, w, k, v, a, b))
    _, ys = lax.scan(step, jnp.zeros((B, H, N, N), jnp.float32), xs)
    return jnp.moveaxis(ys, 0, 1)


def rwkv7_time_mix(h_ctx, h_lat, mix, w_r, w_k, w_v, w_o, dec0, dec1, dec2,
                   iclr0, iclr1, iclr2, gate1, gate2, k_k, k_a, r_k, lnx_w, lnx_b):
    B, C, D = h_ctx.shape
    S = h_lat.shape[1]
    T = C + S
    f32 = jnp.float32
    h = jnp.concatenate([h_ctx, h_lat], axis=1)
    dh = jnp.concatenate([centred_shift_delta(h_ctx), centred_shift_delta(h_lat)], axis=1)
    xr, xw, xk, xv, xa, xg = [h + dh * mix[i] for i in range(6)]

    def heads(t):
        return t.reshape(B, T, RWKV_HEADS, RWKV_HEAD).astype(f32)

    r = heads(xr @ w_r)
    k = heads(xk @ w_k)
    v = heads(xv @ w_v)
    g = jax.nn.sigmoid(xg @ gate1) @ gate2
    kk = k * k_k.reshape(RWKV_HEADS, RWKV_HEAD).astype(f32)
    kk = kk * lax.rsqrt(jnp.maximum(jnp.sum(kk * kk, axis=-1, keepdims=True), 1e-24))
    ka = k_a.reshape(RWKV_HEADS, RWKV_HEAD).astype(f32)
    rk = r_k.astype(f32)
    rev = np.concatenate([np.arange(C)[::-1], C + np.arange(S)[::-1]])
    y = jnp.zeros_like(r)
    bonus = jnp.zeros_like(r)
    for d in range(N_DIRS):
        w_log = -jax.nn.softplus(-(dec0[d] + jnp.tanh(xw @ dec1[d]) @ dec2[d])) - 0.5
        decay = jnp.exp(-jnp.exp(heads(w_log)))
        a = heads(jax.nn.sigmoid(iclr0[d] + (xa @ iclr1[d]) @ iclr2[d]))
        k_d = k * (1.0 + (a - 1.0) * ka)
        inputs = (r, decay, k_d, v, -kk, kk * a)
        if d == 1:
            inputs = tuple(t[:, rev] for t in inputs)
        y_d = rwkv7_scan(*inputs)
        y = y + (y_d[:, rev] if d == 1 else y_d)
        bonus = bonus + jnp.sum(r * k_d * rk, axis=-1, keepdims=True) * v
    mu = jnp.mean(y, axis=-1, keepdims=True)
    var = jnp.mean(jnp.square(y - mu), axis=-1, keepdims=True)
    y = (y - mu) * lax.rsqrt(var + LNX_EPS)
    y = (y.reshape(B, T, D) * lnx_w.astype(f32) + lnx_b.astype(f32)
         + bonus.reshape(B, T, D)).astype(h.dtype)
    out = (y * g) @ w_o
    return out[:, :C], out[:, C:]


def apply_rope(x, cos, sin):
    shp = (x.shape[1],) + (1,) * (x.ndim - 3) + (HEAD_DIM // 2,)
    cos = cos.reshape(shp)
    sin = sin.reshape(shp)
    xp = x.reshape(x.shape[:-1] + (HEAD_DIM // 2, 2)).astype(jnp.float32)
    x0, x1 = xp[..., 0], xp[..., 1]
    out = jnp.stack([x0 * cos - x1 * sin, x0 * sin + x1 * cos], axis=-1)
    return out.reshape(x.shape).astype(x.dtype)


def attend(q, k, v):
    s = jnp.einsum('bqhgd,bkhd->bhgqk', q, k, preferred_element_type=jnp.float32) * HEAD_DIM ** -0.5
    p = jax.nn.softmax(s, axis=-1)
    return jnp.einsum('bhgqk,bkhd->bqhgd', p.astype(v.dtype), v)


def gqa_mix(h_ctx, h_lat, w_q, w_k, w_v, w_o, q_gain, k_gain, cos, sin, with_ctx_out):
    B, S, _ = h_lat.shape
    C = h_ctx.shape[1]

    def queries(h):
        q = (h @ w_q).reshape(B, h.shape[1], N_KV_HEADS, Q_PER_KV, HEAD_DIM)
        return rms_norm(q, q_gain)

    def keys_values(h):
        kx = rms_norm((h @ w_k).reshape(B, h.shape[1], N_KV_HEADS, HEAD_DIM), k_gain)
        vx = (h @ w_v).reshape(B, h.shape[1], N_KV_HEADS, HEAD_DIM)
        return kx, vx

    q_l = apply_rope(queries(h_lat), cos, sin)
    k_l, v_l = keys_values(h_lat)
    k_l = apply_rope(k_l, cos, sin)
    k_c, v_c = keys_values(h_ctx)
    keys = jnp.concatenate([k_l, k_c], axis=1)
    vals = jnp.concatenate([v_l, v_c], axis=1)
    n_blocks = S // Q_BLOCK
    q_blocks = jnp.moveaxis(q_l.reshape(B, n_blocks, Q_BLOCK, N_KV_HEADS, Q_PER_KV, HEAD_DIM), 1, 0)
    o = lax.map(lambda qb: attend(qb, keys, vals), q_blocks)
    o = jnp.moveaxis(o, 0, 1).reshape(B, S, D_ATTN)
    out_lat = o @ w_o
    out_ctx = None
    if with_ctx_out:
        out_ctx = attend(queries(h_ctx), k_c, v_c).reshape(B, C, D_ATTN) @ w_o
    return out_ctx, out_lat


def swiglu(h, w_gate, w_up, w_down):
    return (jax.nn.silu(h @ w_gate) * (h @ w_up)) @ w_down


def moe_swiglu(h, router, w_gate, w_up, w_down):
    logits = (h @ router).astype(jnp.float32)
    top_v, top_i = lax.top_k(logits, TOP_K)
    top_w = jax.nn.softmax(top_v, axis=-1)
    gates = jnp.sum(jax.nn.one_hot(top_i, N_EXPERTS, dtype=jnp.float32) * top_w[..., None], axis=-2)
    out = jnp.zeros_like(h)
    for e in range(N_EXPERTS):
        out = out + gates[..., e:e + 1].astype(h.dtype) * swiglu(h, w_gate[e], w_up[e], w_down[e])
    return out


def setup_inputs(seed: int = 0) -> dict:
    key = jax.random.key(seed)
    ks = iter(jax.random.split(key, 64))
    f32 = jnp.float32
    D = D_MODEL

    def nrm(shape, scale):
        return jax.random.normal(next(ks), shape, f32) * scale

    def gain(shape):
        return 1.0 + nrm(shape, 0.02)

    return {
        'x': nrm((BATCH, SEQ, D), 1.0),
        'c': nrm((BATCH, D), 1.0),
        'ctx': nrm((BATCH, CTX_LEN, D), 1.0),
        'c_ctx': nrm((D,), 1.0),
        'ada_w': nrm((DEPTH, D, N_MOD * D), 0.5 * D ** -0.5),
        'ada_b': nrm((DEPTH, N_MOD * D), 0.01),
        'norm_gains': gain((DEPTH, 2, D)),
        'rwkv_mix': jax.random.uniform(next(ks), (N_EVEN, 6, D), f32),
        'rwkv_wr': nrm((N_EVEN, D, D), D ** -0.5),
        'rwkv_wk': nrm((N_EVEN, D, D), D ** -0.5),
        'rwkv_wv': nrm((N_EVEN, D, D), D ** -0.5),
        'rwkv_wo': nrm((N_EVEN, D, D), D ** -0.5),
        'rwkv_dec0': jax.random.uniform(next(ks), (N_EVEN, N_DIRS, D), f32, -6.0, -1.0),
        'rwkv_dec1': nrm((N_EVEN, N_DIRS, D, DECAY_LORA), D ** -0.5),
        'rwkv_dec2': nrm((N_EVEN, N_DIRS, DECAY_LORA, D), 0.1 * DECAY_LORA ** -0.5),
        'rwkv_iclr0': nrm((N_EVEN, N_DIRS, D), 0.5),
        'rwkv_iclr1': nrm((N_EVEN, N_DIRS, D, ICLR_LORA), D ** -0.5),
        'rwkv_iclr2': nrm((N_EVEN, N_DIRS, ICLR_LORA, D), 0.1 * ICLR_LORA ** -0.5),
        'rwkv_gate1': nrm((N_EVEN, D, GATE_LORA), D ** -0.5),
        'rwkv_gate2': nrm((N_EVEN, GATE_LORA, D), GATE_LORA ** -0.5),
        'rwkv_kk': 1.0 + nrm((N_EVEN, D), 0.1),
        'rwkv_ka': 1.0 + nrm((N_EVEN, D), 0.1),
        'rwkv_rk': nrm((N_EVEN, RWKV_HEADS, RWKV_HEAD), 0.1),
        'rwkv_lnx_w': gain((N_EVEN, D)),
        'rwkv_lnx_b': nrm((N_EVEN, D), 0.01),
        'ffn_w_gate': nrm((N_EVEN, D, D_FF), D ** -0.5),
        'ffn_w_up': nrm((N_EVEN, D, D_FF), D ** -0.5),
        'ffn_w_down': nrm((N_EVEN, D_FF, D), D_FF ** -0.5),
        'attn_wq': nrm((N_ODD, D, D_ATTN), D ** -0.5),
        'attn_wk': nrm((N_ODD, D, D_KV), D ** -0.5),
        'attn_wv': nrm((N_ODD, D, D_KV), D ** -0.5),
        'attn_wo': nrm((N_ODD, D_ATTN, D), D_ATTN ** -0.5),
        'attn_q_gain': gain((N_ODD, HEAD_DIM)),
        'attn_k_gain': gain((N_ODD, HEAD_DIM)),
        'moe_router': nrm((N_ODD, D, N_EXPERTS), D ** -0.5),
        'moe_w_gate': nrm((N_ODD, N_EXPERTS, D, D_FF_EXPERT), D ** -0.5),
        'moe_w_up': nrm((N_ODD, N_EXPERTS, D, D_FF_EXPERT), D ** -0.5),
        'moe_w_down': nrm((N_ODD, N_EXPERTS, D_FF_EXPERT, D), D_FF_EXPERT ** -0.5),
        'final_norm': gain((D,)),
    }


def reference(x, c, ctx, c_ctx, ada_w, ada_b, norm_gains,
              rwkv_mix, rwkv_wr, rwkv_wk, rwkv_wv, rwkv_wo,
              rwkv_dec0, rwkv_dec1, rwkv_dec2, rwkv_iclr0, rwkv_iclr1, rwkv_iclr2,
              rwkv_gate1, rwkv_gate2, rwkv_kk, rwkv_ka, rwkv_rk, rwkv_lnx_w, rwkv_lnx_b,
              ffn_w_gate, ffn_w_up, ffn_w_down,
              attn_wq, attn_wk, attn_wv, attn_wo, attn_q_gain, attn_k_gain,
              moe_router, moe_w_gate, moe_w_up, moe_w_down, final_norm):
    n_lat = x.shape[1]
    rows = n_lat // GRID_W
    row = jnp.repeat(jnp.arange(rows), GRID_W).astype(jnp.float32)
    col = jnp.tile(jnp.arange(GRID_W), rows).astype(jnp.float32)
    inv_freq = ROPE_THETA ** (-jnp.arange(ROPE_PAIRS_PER_AXIS, dtype=jnp.float32) / ROPE_PAIRS_PER_AXIS)
    ang = jnp.concatenate([row[:, None] * inv_freq, col[:, None] * inv_freq], axis=-1)
    cos, sin = jnp.cos(ang), jnp.sin(ang)

    h_lat, h_ctx = x, ctx
    for layer in range(DEPTH):
        last = layer == DEPTH - 1
        j = layer // N_MIXERS
        sh_a, sc_a, g_a, sh_f, sc_f, g_f = ada_params(c, ada_w[layer], ada_b[layer])
        csh_a, csc_a, cg_a, csh_f, csc_f, cg_f = ada_params(c_ctx, ada_w[layer], ada_b[layer])
        n_lat_a = modulate(rms_norm(h_lat, norm_gains[layer, 0]), sh_a, sc_a)
        n_ctx_a = modulate(rms_norm(h_ctx, norm_gains[layer, 0]), csh_a, csc_a)
        if layer % N_MIXERS == 0:
            mix_ctx, mix_lat = rwkv7_time_mix(
                n_ctx_a, n_lat_a, rwkv_mix[j], rwkv_wr[j], rwkv_wk[j], rwkv_wv[j], rwkv_wo[j],
                rwkv_dec0[j], rwkv_dec1[j], rwkv_dec2[j], rwkv_iclr0[j], rwkv_iclr1[j], rwkv_iclr2[j],
                rwkv_gate1[j], rwkv_gate2[j], rwkv_kk[j], rwkv_ka[j], rwkv_rk[j],
                rwkv_lnx_w[j], rwkv_lnx_b[j])
        else:
            mix_ctx, mix_lat = gqa_mix(
                n_ctx_a, n_lat_a, attn_wq[j], attn_wk[j], attn_wv[j], attn_wo[j],
                attn_q_gain[j], attn_k_gain[j], cos, sin, not last)

        def channel_mix(h):
            if layer % 2 == 0:
                return swiglu(h, ffn_w_gate[j], ffn_w_up[j], ffn_w_down[j])
            return moe_swiglu(h, moe_router[j], moe_w_gate[j], moe_w_up[j], moe_w_down[j])

        h_lat = h_lat + g_a * mix_lat
        h_lat = h_lat + g_f * channel_mix(modulate(rms_norm(h_lat, norm_gains[layer, 1]), sh_f, sc_f))
        if not last:
            h_ctx = h_ctx + cg_a * mix_ctx
            h_ctx = h_ctx + cg_f * channel_mix(modulate(rms_norm(h_ctx, norm_gains[layer, 1]), csh_f, csc_f))
    return rms_norm(h_lat, final_norm)
```

```python
import functools
import math

import numpy as np
import jax
import jax.numpy as jnp
from jax import lax
from jax.experimental import pallas as pl
from jax.experimental.pallas import tpu as pltpu

F32 = jnp.float32
BF16 = jnp.bfloat16

N_MOD = 6
NORM_EPS = 1e-6
LNX_EPS = 64e-5
RWKV_HEAD = 64
HEAD_DIM = 128
GRID_W = 64
ROPE_THETA = 10000.0
TOP_K = 2
LANES = 128
SUBLANES = 8
VMEM_LIMIT = 56 * 1024 * 1024


def _pick(n, target, mult=SUBLANES):
    best = None
    for b in range(mult, min(n, target) + 1, mult):
        if n % b == 0:
            best = b
    return best if best is not None else n


def _params(*sem):
    return pltpu.CompilerParams(dimension_semantics=sem, vmem_limit_bytes=VMEM_LIMIT)


def _sigmoid(x):
    return 1.0 / (1.0 + jnp.exp(-x))


def _silu(x):
    return x * _sigmoid(x)


def _bdot(a, b):
    return jnp.dot(a, b, preferred_element_type=F32)


def _ada_kernel(c_ref, w_ref, b_ref, o_ref):
    s = _silu(c_ref[...]).astype(BF16)
    o_ref[...] = _bdot(s, w_ref[...].astype(BF16)) + b_ref[...]


def ada_params(cond8, ada_w, ada_b):
    depth, d, n = ada_w.shape
    bn = _pick(n, 512, LANES)
    return pl.pallas_call(
        _ada_kernel,
        grid=(depth, n // bn),
        in_specs=[
            pl.BlockSpec((SUBLANES, d), lambda l, j: (0, 0)),
            pl.BlockSpec((None, d, bn), lambda l, j: (l, 0, j)),
            pl.BlockSpec((None, 1, bn), lambda l, j: (l, 0, j)),
        ],
        out_specs=pl.BlockSpec((None, SUBLANES, bn), lambda l, j: (l, 0, j)),
        out_shape=jax.ShapeDtypeStruct((depth, SUBLANES, n), F32),
        compiler_params=_params("arbitrary", "arbitrary"),
        name="ada_params",
    )(cond8, ada_w, ada_b.reshape(depth, 1, n))


def _norm_mod(x, gain, shift, scale):
    ms = jnp.mean(x * x, axis=-1, keepdims=True)
    return (x * lax.rsqrt(ms + NORM_EPS) * gain) * (1.0 + scale) + shift


def _norm_mod_kernel(h_ref, g_ref, m_ref, o_ref, *, sub):
    n = _norm_mod(h_ref[...], g_ref[...], m_ref[3 * sub:3 * sub + 1, :],
                  m_ref[3 * sub + 1:3 * sub + 2, :])
    o_ref[...] = n.astype(o_ref.dtype)


def norm_mod(h, gain, mod, *, rows, n_lat, sub):
    t, d = h.shape
    bm = _pick(math.gcd(n_lat, t - n_lat) if t > n_lat else n_lat, 256)
    nlat_blocks = n_lat // bm
    return pl.pallas_call(
        functools.partial(_norm_mod_kernel, sub=sub),
        grid=(rows // bm,),
        in_specs=[
            pl.BlockSpec((bm, d), lambda i: (i, 0)),
            pl.BlockSpec((1, d), lambda i: (0, 0)),
            pl.BlockSpec((None, N_MOD, d), lambda i: (jnp.where(i >= nlat_blocks, 1, 0), 0, 0)),
        ],
        out_specs=pl.BlockSpec((bm, d), lambda i: (i, 0)),
        out_shape=jax.ShapeDtypeStruct((rows, d), BF16),
        compiler_params=_params("arbitrary"),
        name="norm_mod",
    )(h, gain.reshape(1, d), mod)


def _norm_router_kernel(h_ref, g_ref, m_ref, r_ref, o_ref, gate_ref, *, n_experts):
    n = _norm_mod(h_ref[...], g_ref[...], m_ref[3:4, :], m_ref[4:5, :])
    o_ref[...] = n.astype(o_ref.dtype)
    logits = jnp.dot(n, r_ref[...], precision=lax.Precision.HIGHEST,
                     preferred_element_type=F32)
    lane = lax.broadcasted_iota(jnp.int32, logits.shape, 1)
    neg = jnp.float32(-jnp.inf)
    lg = jnp.where(lane < n_experts, logits, neg)
    m1 = jnp.max(lg, axis=-1, keepdims=True)
    i1 = jnp.min(jnp.where(lg == m1, lane, LANES), axis=-1, keepdims=True)
    lg2 = jnp.where(lane == i1, neg, lg)
    m2 = jnp.max(lg2, axis=-1, keepdims=True)
    i2 = jnp.min(jnp.where(lg2 == m2, lane, LANES), axis=-1, keepdims=True)
    e = jnp.exp(m2 - m1)
    w1 = 1.0 / (1.0 + e)
    gate_ref[...] = jnp.where(lane == i1, w1, 0.0) + jnp.where(lane == i2, 1.0 - w1, 0.0)


def norm_router(h, gain, mod_lat, router_pad, *, rows, n_experts):
    t, d = h.shape
    bm = _pick(rows, 256)
    return pl.pallas_call(
        functools.partial(_norm_router_kernel, n_experts=n_experts),
        grid=(rows // bm,),
        in_specs=[
            pl.BlockSpec((bm, d), lambda i: (i, 0)),
            pl.BlockSpec((1, d), lambda i: (0, 0)),
            pl.BlockSpec((N_MOD, d), lambda i: (0, 0)),
            pl.BlockSpec((d, LANES), lambda i: (0, 0)),
        ],
        out_specs=[pl.BlockSpec((bm, d), lambda i: (i, 0)),
                   pl.BlockSpec((bm, LANES), lambda i: (i, 0))],
        out_shape=[jax.ShapeDtypeStruct((rows, d), BF16),
                   jax.ShapeDtypeStruct((rows, LANES), F32)],
        compiler_params=_params("arbitrary"),
        name="norm_router",
    )(h, gain.reshape(1, d), mod_lat, router_pad)


def _final_norm_kernel(h_ref, g_ref, o_ref):
    x = h_ref[...]
    ms = jnp.mean(x * x, axis=-1, keepdims=True)
    o_ref[...] = x * lax.rsqrt(ms + NORM_EPS) * g_ref[...]


def final_rms_norm(h, gain, *, rows):
    t, d = h.shape
    bm = _pick(rows, 256)
    return pl.pallas_call(
        _final_norm_kernel,
        grid=(rows // bm,),
        in_specs=[pl.BlockSpec((bm, d), lambda i: (i, 0)),
                  pl.BlockSpec((1, d), lambda i: (0, 0))],
        out_specs=pl.BlockSpec((bm, d), lambda i: (i, 0)),
        out_shape=jax.ShapeDtypeStruct((rows, d), F32),
        compiler_params=_params("arbitrary"),
        name="final_norm",
    )(h, gain.reshape(1, d))


def _rwkv_prep_kernel(h_ref, hp_ref, hn_ref, g_ref, m_ref, mix_ref, *o_refs,
                      nlat_blocks, nblocks):
    i = pl.program_id(0)
    gain, shift, scale = g_ref[...], m_ref[0:1, :], m_ref[1:2, :]
    n = _norm_mod(h_ref[...], gain, shift, scale)
    n_prev = _norm_mod(hp_ref[...], gain, shift, scale)[SUBLANES - 1:SUBLANES, :]
    n_next = _norm_mod(hn_ref[...], gain, shift, scale)[0:1, :]
    first = jnp.logical_or(i == 0, i == nlat_blocks)
    last = jnp.logical_or(i == nlat_blocks - 1, i == nblocks - 1)
    n_prev = jnp.where(first, 0.0, n_prev)
    n_next = jnp.where(last, 0.0, n_next)
    bm = n.shape[0]
    row = lax.broadcasted_iota(jnp.int32, n.shape, 0)
    prev = jnp.where(row == 0, n_prev, pltpu.roll(n, 1, 0))
    nxt = jnp.where(row == bm - 1, n_next, pltpu.roll(n, bm - 1, 0))
    dh = 0.5 * (prev + nxt) - n
    for j, o_ref in enumerate(o_refs):
        o_ref[...] = (n + dh * mix_ref[j:j + 1, :]).astype(o_ref.dtype)


def rwkv_prep(h, gain, mod, mix, *, n_lat):
    t, d = h.shape
    bm = _pick(math.gcd(n_lat, t - n_lat), 128)
    nblocks, nlat_blocks = t // bm, n_lat // bm
    r8 = bm // SUBLANES
    n8 = t // SUBLANES
    return pl.pallas_call(
        functools.partial(_rwkv_prep_kernel, nlat_blocks=nlat_blocks, nblocks=nblocks),
        grid=(nblocks,),
        in_specs=[
            pl.BlockSpec((bm, d), lambda i: (i, 0)),
            pl.BlockSpec((SUBLANES, d), lambda i: (jnp.maximum(i * r8 - 1, 0), 0)),
            pl.BlockSpec((SUBLANES, d), lambda i: (jnp.minimum((i + 1) * r8, n8 - 1), 0)),
            pl.BlockSpec((1, d), lambda i: (0, 0)),
            pl.BlockSpec((None, N_MOD, d), lambda i: (jnp.where(i >= nlat_blocks, 1, 0), 0, 0)),
            pl.BlockSpec((6, d), lambda i: (0, 0)),
        ],
        out_specs=[pl.BlockSpec((bm, d), lambda i: (i, 0))] * 6,
        out_shape=[jax.ShapeDtypeStruct((t, d), BF16)] * 6,
        compiler_params=_params("arbitrary"),
        name="rwkv_prep",
    )(h, h, h, gain.reshape(1, d), mod, mix)


def _mm_kernel(*refs, nk, n_extra, epi):
    x_ref, w_ref = refs[0], refs[1]
    extras = refs[2:2 + n_extra]
    o_ref = refs[2 + n_extra]
    i_m = pl.program_id(1)
    part = _bdot(x_ref[...], w_ref[...].astype(BF16))
    if nk == 1:
        o_ref[...] = epi(part, extras, i_m).astype(o_ref.dtype)
    else:
        acc_ref = refs[3 + n_extra]
        k = pl.program_id(2)

        @pl.when(k == 0)
        def _():
            acc_ref[...] = part

        @pl.when(k > 0)
        def _():
            acc_ref[...] += part

        @pl.when(k == nk - 1)
        def _():
            o_ref[...] = epi(acc_ref[...], extras, i_m).astype(o_ref.dtype)


def matmul(x, w, *, rows, lead=(), out_dtype=F32, epi=None, extras=(), bm=1056, bn=512,
           bk=4096, name="matmul"):
    kdim = x.shape[1]
    n = w.shape[-1]
    bm = _pick(rows, bm, 16)
    bn = _pick(n, bn, LANES)
    bk = _pick(kdim, bk, LANES)
    nk = kdim // bk
    if epi is None:
        epi = lambda acc, ex, i_m: acc
    nl = len(lead)
    in_specs = [
        pl.BlockSpec((bm, bk), lambda j, i, k: (i, k)),
        pl.BlockSpec((None,) * nl + (bk, bn), lambda j, i, k: tuple(lead) + (k, j)),
    ]
    args = [x, w]
    for arr, bshape, imap in extras:
        in_specs.append(pl.BlockSpec(bshape, imap))
        args.append(arr)
    return pl.pallas_call(
        functools.partial(_mm_kernel, nk=nk, n_extra=len(extras), epi=epi),
        grid=(n // bn, rows // bm, nk),
        in_specs=in_specs,
        out_specs=pl.BlockSpec((bm, bn), lambda j, i, k: (i, j)),
        out_shape=jax.ShapeDtypeStruct((rows, n), out_dtype),
        scratch_shapes=[pltpu.VMEM((bm, bn), F32)] if nk > 1 else [],
        compiler_params=_params("arbitrary", "arbitrary", "arbitrary"),
        name=name,
    )(*args)


def _residual_epi(n_lat, gate_idx):
    def epi(acc, ex, i_m):
        h_ref, m_ref = ex
        bm = acc.shape[0]
        row = i_m * bm + lax.broadcasted_iota(jnp.int32, (bm, 1), 0)
        gate = jnp.where(row < n_lat, m_ref[0, gate_idx:gate_idx + 1, :],
                         m_ref[1, gate_idx:gate_idx + 1, :])
        return h_ref[...] + gate * acc
    return epi


def residual_matmul(x, w, h, mod, *, rows, n_lat, gate_idx, lead=(), bk=4096, name="residual_mm"):
    n = w.shape[-1]
    bm = _pick(rows, 1056, 16)
    bn = _pick(n, 512, LANES)
    extras = [
        (h, (bm, bn), lambda j, i, k: (i, j)),
        (mod, (2, N_MOD, bn), lambda j, i, k: (0, 0, j)),
    ]
    return matmul(x, w, rows=rows, lead=lead, epi=_residual_epi(n_lat, gate_idx),
                  extras=extras, bm=bm, bn=bn, bk=bk, name=name)


def _glu_kernel(x_ref, wg_ref, wu_ref, *rest, nper, gated):
    x = x_ref[...]
    g = _bdot(x, wg_ref[...].astype(BF16))
    u = _bdot(x, wu_ref[...].astype(BF16))
    hdn = _silu(g) * u
    if gated:
        gate_ref, o_ref = rest
        e = pl.program_id(0) // nper
        gates = gate_ref[...]
        lane = lax.broadcasted_iota(jnp.int32, gates.shape, 1)
        hdn = hdn * jnp.sum(jnp.where(lane == e, gates, 0.0), axis=-1, keepdims=True)
    else:
        (o_ref,) = rest
    o_ref[...] = hdn.astype(o_ref.dtype)


def glu(x, wg, wu, *, rows, gates=None):
    n_exp, kdim, f = wg.shape
    bm = _pick(rows, 1056, 16)
    bn = _pick(f, 256, LANES)
    nper = f // bn
    in_specs = [
        pl.BlockSpec((bm, kdim), lambda j, i: (i, 0)),
        pl.BlockSpec((None, kdim, bn), lambda j, i: (j // nper, 0, j % nper)),
        pl.BlockSpec((None, kdim, bn), lambda j, i: (j // nper, 0, j % nper)),
    ]
    args = [x, wg, wu]
    if gates is not None:
        in_specs.append(pl.BlockSpec((bm, LANES), lambda j, i: (i, 0)))
        args.append(gates)
    return pl.pallas_call(
        functools.partial(_glu_kernel, nper=nper, gated=gates is not None),
        grid=(n_exp * nper, rows // bm),
        in_specs=in_specs,
        out_specs=pl.BlockSpec((bm, bn), lambda j, i: (i, j)),
        out_shape=jax.ShapeDtypeStruct((rows, n_exp * f), BF16),
        compiler_params=_params("arbitrary", "arbitrary"),
        name="glu",
    )(*args)


def _seg_sum(x, seg_ref):
    hi = x.astype(BF16)
    lo = (x - hi.astype(F32)).astype(BF16)
    return _bdot(hi, seg_ref[...]) + _bdot(lo, seg_ref[...])


def _rwkv_elem_kernel(t_ref, dec2_ref, iclr2_ref, dec0_ref, iclr0_ref, r_ref, k_ref, v_ref,
                      kk_ref, ka_ref, rk_ref, seg_ref,
                      w0_ref, w1_ref, k0_ref, k1_ref, b0_ref, b1_ref, nkk_ref, bonus_ref,
                      *, rd, ri):
    r, k, v = r_ref[...], k_ref[...], v_ref[...]
    kk = k * kk_ref[...]
    kk = kk * lax.rsqrt(jnp.maximum(_seg_sum(kk * kk, seg_ref), 1e-24))
    nkk_ref[...] = -kk
    tt = t_ref[...]
    bonus = jnp.zeros_like(v)
    outs = ((w0_ref, k0_ref, b0_ref), (w1_ref, k1_ref, b1_ref))
    for d, (w_ref, kd_ref, b_ref) in enumerate(outs):
        z = dec0_ref[d:d + 1, :] + _bdot(tt[:, d * rd:(d + 1) * rd], dec2_ref[d])
        w_log = -(jnp.maximum(-z, 0.0) + jnp.log(1.0 + jnp.exp(-jnp.abs(z)))) - 0.5
        w_ref[...] = jnp.exp(-jnp.exp(w_log))
        ta = tt[:, 2 * rd + d * ri:2 * rd + (d + 1) * ri]
        a = _sigmoid(iclr0_ref[d:d + 1, :] + _bdot(ta, iclr2_ref[d]))
        k_d = k * (1.0 + (a - 1.0) * ka_ref[...])
        kd_ref[...] = k_d
        b_ref[...] = kk * a
        bonus = bonus + _seg_sum(r * k_d * rk_ref[...], seg_ref) * v
    bonus_ref[...] = bonus


def rwkv_elem(tt, dec2, iclr2, dec0, iclr0, r, k, v, kk, ka, rk, seg, *, rd, ri):
    t, d = r.shape
    bm = _pick(t, 1056, 16)
    bc = seg.shape[0]
    tw = tt.shape[1]
    tile = pl.BlockSpec((bm, bc), lambda j, i: (i, j))
    row = pl.BlockSpec((1, bc), lambda j, i: (0, j))
    return pl.pallas_call(
        functools.partial(_rwkv_elem_kernel, rd=rd, ri=ri),
        grid=(d // bc, t // bm),
        in_specs=[
            pl.BlockSpec((bm, tw), lambda j, i: (i, 0)),
            pl.BlockSpec((2, rd, bc), lambda j, i: (0, 0, j)),
            pl.BlockSpec((2, ri, bc), lambda j, i: (0, 0, j)),
            pl.BlockSpec((2, bc), lambda j, i: (0, j)),
            pl.BlockSpec((2, bc), lambda j, i: (0, j)),
            tile, tile, tile, row, row, row,
            pl.BlockSpec((bc, bc), lambda j, i: (0, 0)),
        ],
        out_specs=[tile] * 8,
        out_shape=[jax.ShapeDtypeStruct((t, d), F32)] * 8,
        compiler_params=_params("arbitrary", "arbitrary"),
        name="rwkv_elem",
    )(tt, dec2, iclr2, dec0, iclr0, r, k, v, kk, ka, rk, seg)


def _scan_kernel(r_ref, w_ref, k_ref, v_ref, a_ref, b_ref, y_ref, s_ref, *, tb, n):
    @pl.when(pl.program_id(0) == 0)
    def _():
        s_ref[...] = jnp.zeros_like(s_ref)

    nvb = n // SUBLANES
    lanes = s_ref.shape[-1]

    def bcast(ref, t, kc):
        return jnp.broadcast_to(ref[t, pl.ds(kc, 1), :], (SUBLANES, lanes))

    def vslice(vb):
        return pl.ds(vb * SUBLANES, SUBLANES)

    def step(t, carry):
        sa = [jnp.zeros((SUBLANES, lanes), F32) for _ in range(nvb)]
        for kc in range(n):
            ab = bcast(a_ref, t, kc)
            for vb in range(nvb):
                sa[vb] = sa[vb] + s_ref[kc, vslice(vb), :] * ab
        vv = [v_ref[t, vslice(vb), :] for vb in range(nvb)]
        y = [jnp.zeros((SUBLANES, lanes), F32) for _ in range(nvb)]
        for kc in range(n):
            wb, bb = bcast(w_ref, t, kc), bcast(b_ref, t, kc)
            kb, rb = bcast(k_ref, t, kc), bcast(r_ref, t, kc)
            for vb in range(nvb):
                s_new = s_ref[kc, vslice(vb), :] * wb + sa[vb] * bb + vv[vb] * kb
                s_ref[kc, vslice(vb), :] = s_new
                y[vb] = y[vb] + s_new * rb
        for vb in range(nvb):
            y_ref[t, vslice(vb), :] = y[vb]
        return carry

    lax.fori_loop(0, tb, step, 0)


def rwkv_scan(r, w, k, v, a, b):
    t, n, lanes = r.shape
    tb = _pick(t, 32, 1)
    blk = pl.BlockSpec((tb, n, lanes), lambda i: (i, 0, 0))
    return pl.pallas_call(
        functools.partial(_scan_kernel, tb=tb, n=n),
        grid=(t // tb,),
        in_specs=[blk] * 6,
        out_specs=blk,
        out_shape=jax.ShapeDtypeStruct((t, n, lanes), F32),
        scratch_shapes=[pltpu.VMEM((n, n, lanes), F32)],
        compiler_params=_params("arbitrary"),
        name="rwkv_scan",
    )(r, w, k, v, a, b)


def _rwkv_post_kernel(yf_ref, yb_ref, bonus_ref, g_ref, lw_ref, lb_ref, seg_ref, o_ref, *, n):
    y = yf_ref[...] + yb_ref[...]
    mu = _seg_sum(y, seg_ref) * (1.0 / n)
    yc = y - mu
    var = _seg_sum(yc * yc, seg_ref) * (1.0 / n)
    yn = yc * lax.rsqrt(var + LNX_EPS)
    out = (yn * lw_ref[...] + lb_ref[...] + bonus_ref[...]).astype(BF16)
    o_ref[...] = (out * g_ref[...]).astype(o_ref.dtype)


def rwkv_post(yf, yb, bonus, g, lnx_w, lnx_b, seg):
    t, d = yf.shape
    bm = _pick(t, 1056, 16)
    bc = seg.shape[0]
    tile = pl.BlockSpec((bm, bc), lambda j, i: (i, j))
    row = pl.BlockSpec((1, bc), lambda j, i: (0, j))
    return pl.pallas_call(
        functools.partial(_rwkv_post_kernel, n=RWKV_HEAD),
        grid=(d // bc, t // bm),
        in_specs=[tile, tile, tile, tile, row, row, pl.BlockSpec((bc, bc), lambda j, i: (0, 0))],
        out_specs=tile,
        out_shape=jax.ShapeDtypeStruct((t, d), BF16),
        compiler_params=_params("arbitrary", "arbitrary"),
        name="rwkv_post",
    )(yf, yb, bonus, g, lnx_w.reshape(1, d), lnx_b.reshape(1, d), seg)


def _qk_epi(scale):
    def epi(acc, ex, i_m):
        gain_ref, cos_ref, sin_ref = ex
        cos, sin = cos_ref[...], sin_ref[...]
        gain = gain_ref[...] * scale
        lane = lax.broadcasted_iota(jnp.int32, cos.shape, 1)
        even = (lane % 2) == 0
        outs = []
        for j in range(acc.shape[1] // HEAD_DIM):
            q = acc[:, j * HEAD_DIM:(j + 1) * HEAD_DIM]
            ms = jnp.mean(q * q, axis=-1, keepdims=True)
            q = q * lax.rsqrt(ms + NORM_EPS) * gain
            partner = jnp.where(even, pltpu.roll(q, HEAD_DIM - 1, 1), pltpu.roll(q, 1, 1))
            outs.append(q * cos + partner * sin)
        return jnp.concatenate(outs, axis=1)
    return epi


def qk_matmul(x, w, gain, cos_t, sin_t, *, rows, scale, lead=(), name="qk_mm"):
    n = w.shape[-1]
    bm = _pick(rows, 1024, 16)
    bn = _pick(n, 512, LANES)
    extras = [
        (gain.reshape(1, HEAD_DIM), (1, HEAD_DIM), lambda j, i, k: (0, 0)),
        (cos_t, (bm, HEAD_DIM), lambda j, i, k: (i, 0)),
        (sin_t, (bm, HEAD_DIM), lambda j, i, k: (i, 0)),
    ]
    return matmul(x, w, rows=rows, lead=lead, out_dtype=BF16, epi=_qk_epi(scale),
                  extras=extras, bm=bm, bn=bn, name=name)


def _attn_kernel(q_ref, k_ref, v_ref, o_ref, *, groups, bk, nchunks):
    bq = q_ref.shape[0]
    q = jnp.concatenate([q_ref[:, g * HEAD_DIM:(g + 1) * HEAD_DIM] for g in range(groups)], axis=0)
    rows = groups * bq

    def body(c, carry):
        m, l, acc = carry
        start = pl.multiple_of(c * bk, bk)
        kc = k_ref[pl.ds(start, bk), :]
        vc = v_ref[pl.ds(start, bk), :]
        s = lax.dot_general(q, kc, (((1,), (1,)), ((), ())), preferred_element_type=F32)
        m_new = jnp.maximum(m, jnp.max(s, axis=1, keepdims=True))
        alpha = jnp.exp(m - m_new)
        p = jnp.exp(s - m_new)
        l = alpha * l + jnp.sum(p, axis=1, keepdims=True)
        acc = alpha * acc + _bdot(p.astype(BF16), vc)
        return m_new, l, acc

    init = (jnp.full((rows, 1), -jnp.inf, F32), jnp.zeros((rows, 1), F32),
            jnp.zeros((rows, HEAD_DIM), F32))
    _, l, acc = lax.fori_loop(0, nchunks, body, init)
    out = acc / l
    for g in range(groups):
        o_ref[:, g * HEAD_DIM:(g + 1) * HEAD_DIM] = out[g * bq:(g + 1) * bq].astype(o_ref.dtype)


def attention(q, k, v, *, groups):
    s, dq = q.shape
    tk, dkv = k.shape
    n_kv = dkv // HEAD_DIM
    bq = _pick(s, 256, 16)
    bk = _pick(tk, 768, 256) if tk % 256 == 0 else tk
    return pl.pallas_call(
        functools.partial(_attn_kernel, groups=groups, bk=bk, nchunks=tk // bk),
        grid=(n_kv, s // bq),
        in_specs=[
            pl.BlockSpec((bq, groups * HEAD_DIM), lambda h, i: (i, h)),
            pl.BlockSpec((tk, HEAD_DIM), lambda h, i: (0, h)),
            pl.BlockSpec((tk, HEAD_DIM), lambda h, i: (0, h)),
        ],
        out_specs=pl.BlockSpec((bq, groups * HEAD_DIM), lambda h, i: (i, h)),
        out_shape=jax.ShapeDtypeStruct((s, dq), BF16),
        compiler_params=_params("arbitrary", "arbitrary"),
        name="attention",
    )(q, k, v)


def _to_scan(x_f, x_b, n_lat):
    t, d = x_f.shape
    h = d // RWKV_HEAD
    fwd = jnp.concatenate([x_f[n_lat:], x_f[:n_lat]], axis=0).reshape(t, h, RWKV_HEAD)
    bwd = jnp.concatenate([x_b[n_lat:][::-1], x_b[:n_lat][::-1]], axis=0).reshape(t, h, RWKV_HEAD)
    return jnp.concatenate([fwd, bwd], axis=1).transpose(0, 2, 1)


def _from_scan(y, n_lat):
    t, n, lanes = y.shape
    h = lanes // 2
    n_ctx = t - n_lat
    yt = y.transpose(0, 2, 1)
    fwd = yt[:, :h].reshape(t, h * n)
    bwd = yt[:, h:].reshape(t, h * n)
    y_f = jnp.concatenate([fwd[n_ctx:], fwd[:n_ctx]], axis=0)
    y_b = jnp.concatenate([bwd[n_ctx:][::-1], bwd[:n_ctx][::-1]], axis=0)
    return y_f, y_b


def _seg_matrix(bc, n):
    idx = np.arange(bc) // n
    return jnp.asarray((idx[:, None] == idx[None, :]).astype(np.float32), dtype=BF16)


def _pad_cols(w, mult):
    pad = (-w.shape[-1]) % mult
    return jnp.pad(w, [(0, 0)] * (w.ndim - 1) + [(0, pad)]) if pad else w


def kernel(x, c, ctx, c_ctx, ada_w, ada_b, norm_gains, rwkv_mix, rwkv_wr, rwkv_wk, rwkv_wv, rwkv_wo, rwkv_dec0, rwkv_dec1, rwkv_dec2, rwkv_iclr0, rwkv_iclr1, rwkv_iclr2, rwkv_gate1, rwkv_gate2, rwkv_kk, rwkv_ka, rwkv_rk, rwkv_lnx_w, rwkv_lnx_b, ffn_w_gate, ffn_w_up, ffn_w_down, attn_wq, attn_wk, attn_wv, attn_wo, attn_q_gain, attn_k_gain, moe_router, moe_w_gate, moe_w_up, moe_w_down, final_norm):
    batch, n_lat, d = x.shape
    n_ctx = ctx.shape[1]
    depth = ada_w.shape[0]
    assert batch == 1 and depth == 2 and rwkv_dec0.shape[1] == 2
    t = n_lat + n_ctx
    n_experts = moe_router.shape[-1]
    assert n_experts <= LANES and TOP_K == 2

    h = jnp.concatenate([x[0], ctx[0]], axis=0)

    cond8 = jnp.zeros((SUBLANES, d), F32).at[0].set(c[0]).at[1].set(c_ctx)
    mod = ada_params(cond8, ada_w, ada_b)[:, :2, :].reshape(depth, 2, N_MOD, d)

    xr, xw, xk, xv, xa, xg = rwkv_prep(h, norm_gains[0, 0], mod[0], rwkv_mix[0], n_lat=n_lat)
    r = matmul(xr, rwkv_wr, rows=t, lead=(0,), name="rwkv_r")
    k = matmul(xk, rwkv_wk, rows=t, lead=(0,), name="rwkv_k")
    v = matmul(xv, rwkv_wv, rows=t, lead=(0,), name="rwkv_v")

    rd, ri = rwkv_dec1.shape[-1], rwkv_iclr1.shape[-1]
    dec1 = jnp.concatenate([rwkv_dec1[0, 0], rwkv_dec1[0, 1]], axis=1)
    iclr1 = jnp.concatenate([rwkv_iclr1[0, 0], rwkv_iclr1[0, 1]], axis=1)
    gate1 = _pad_cols(rwkv_gate1[0], LANES)
    rg = gate1.shape[1]
    gate2 = jnp.pad(rwkv_gate2[0], ((0, rg - rwkv_gate2.shape[1]), (0, 0)))
    tw = matmul(xw, dec1, rows=t, out_dtype=BF16, epi=lambda a, e, i: jnp.tanh(a), name="lora_w")
    ta = matmul(xa, iclr1, rows=t, out_dtype=BF16, name="lora_a")
    tg = matmul(xg, gate1, rows=t, out_dtype=BF16, epi=lambda a, e, i: _sigmoid(a), name="lora_g")
    g = matmul(tg, gate2, rows=t, out_dtype=BF16, name="rwkv_gate")

    bc = _pick(d, 256, RWKV_HEAD)
    seg = _seg_matrix(bc, RWKV_HEAD)
    tt = jnp.concatenate([tw, ta], axis=1)
    w0, w1, k0, k1, b0, b1, nkk, bonus = rwkv_elem(
        tt, rwkv_dec2[0].astype(BF16), rwkv_iclr2[0].astype(BF16), rwkv_dec0[0], rwkv_iclr0[0],
        r, k, v, rwkv_kk[0].reshape(1, d), rwkv_ka[0].reshape(1, d), rwkv_rk[0].reshape(1, d), seg,
        rd=rd, ri=ri)

    y = rwkv_scan(_to_scan(r, r, n_lat), _to_scan(w0, w1, n_lat), _to_scan(k0, k1, n_lat),
                  _to_scan(v, v, n_lat), _to_scan(nkk, nkk, n_lat), _to_scan(b0, b1, n_lat))
    y_f, y_b = _from_scan(y, n_lat)
    yo = rwkv_post(y_f, y_b, bonus, g, rwkv_lnx_w[0], rwkv_lnx_b[0], seg)
    h = residual_matmul(yo, rwkv_wo, h, mod[0], rows=t, n_lat=n_lat, gate_idx=2, lead=(0,),
                        name="rwkv_out")

    nf = norm_mod(h, norm_gains[0, 1], mod[0], rows=t, n_lat=n_lat, sub=1)
    hid = glu(nf, ffn_w_gate, ffn_w_up, rows=t)
    h = residual_matmul(hid, ffn_w_down.astype(BF16), h, mod[0], rows=t, n_lat=n_lat, gate_idx=5,
                        lead=(0,), bk=_pick(ffn_w_down.shape[1], 5504, LANES), name="ffn_down")

    na = norm_mod(h, norm_gains[1, 0], mod[1], rows=t, n_lat=n_lat, sub=0)
    pairs = HEAD_DIM // 4
    inv_freq = ROPE_THETA ** (-jnp.arange(pairs, dtype=F32) / pairs)
    pos = jnp.arange(n_lat)
    ang = jnp.concatenate([(pos // GRID_W).astype(F32)[:, None] * inv_freq,
                           (pos % GRID_W).astype(F32)[:, None] * inv_freq], axis=-1)
    sign = jnp.tile(jnp.asarray([-1.0, 1.0], F32), HEAD_DIM // 2)
    cos_t = jnp.concatenate([jnp.repeat(jnp.cos(ang), 2, axis=-1),
                             jnp.ones((n_ctx, HEAD_DIM), F32)], axis=0)
    sin_t = jnp.concatenate([jnp.repeat(jnp.sin(ang), 2, axis=-1) * sign,
                             jnp.zeros((n_ctx, HEAD_DIM), F32)], axis=0)
    q = qk_matmul(na, attn_wq, attn_q_gain[0], cos_t, sin_t, rows=n_lat,
                  scale=HEAD_DIM ** -0.5, lead=(0,), name="attn_q")
    kx = qk_matmul(na, attn_wk, attn_k_gain[0], cos_t, sin_t, rows=t, scale=1.0, lead=(0,),
                   name="attn_k")
    vx = matmul(na, attn_wv, rows=t, lead=(0,), out_dtype=BF16, name="attn_v")
    groups = attn_wq.shape[-1] // attn_wk.shape[-1]
    o = attention(q, kx, vx, groups=groups)
    h = residual_matmul(o, attn_wo, h, mod[1], rows=n_lat, n_lat=n_lat, gate_idx=2, lead=(0,),
                        name="attn_out")

    nm, gates = norm_router(h, norm_gains[1, 1], mod[1, 0], _pad_cols(moe_router[0], LANES),
                            rows=n_lat, n_experts=n_experts)
    hid = glu(nm, moe_w_gate[0], moe_w_up[0], rows=n_lat, gates=gates)
    wd = moe_w_down[0].reshape(-1, d)
    h = residual_matmul(hid, wd, h, mod[1], rows=n_lat, n_lat=n_lat, gate_idx=5, name="moe_down")

    return final_rms_norm(h, final_norm, rows=n_lat)[None]
```

```python
import functools
import math

import numpy as np
import jax
import jax.numpy as jnp
from jax import lax
from jax.experimental import pallas as pl
from jax.experimental.pallas import tpu as pltpu

F32 = jnp.float32
BF16 = jnp.bfloat16

N_MOD = 6
NORM_EPS = 1e-6
LNX_EPS = 64e-5
RWKV_HEAD = 64
HEAD_DIM = 128
GRID_W = 64
ROPE_THETA = 10000.0
TOP_K = 2
LANES = 128
SUBLANES = 8
VMEM_LIMIT = 56 * 1024 * 1024


def _pick(n, target, mult=SUBLANES):
    best = None
    for b in range(mult, min(n, target) + 1, mult):
        if n % b == 0:
            best = b
    return best if best is not None else n


def _params(*sem):
    return pltpu.CompilerParams(dimension_semantics=sem, vmem_limit_bytes=VMEM_LIMIT)


def _sigmoid(x):
    return 1.0 / (1.0 + jnp.exp(-x))


def _silu(x):
    return x * _sigmoid(x)


def _bdot(a, b):
    return jnp.dot(a, b, preferred_element_type=F32)


def _ada_kernel(c_ref, w_ref, b_ref, o_ref):
    s = _silu(c_ref[...]).astype(BF16)
    o_ref[...] = _bdot(s, w_ref[...].astype(BF16)) + b_ref[...]


def ada_params(cond8, ada_w, ada_b):
    depth, d, n = ada_w.shape
    bn = _pick(n, 512, LANES)
    return pl.pallas_call(
        _ada_kernel,
        grid=(depth, n // bn),
        in_specs=[
            pl.BlockSpec((SUBLANES, d), lambda l, j: (0, 0)),
            pl.BlockSpec((None, d, bn), lambda l, j: (l, 0, j)),
            pl.BlockSpec((None, 1, bn), lambda l, j: (l, 0, j)),
        ],
        out_specs=pl.BlockSpec((None, SUBLANES, bn), lambda l, j: (l, 0, j)),
        out_shape=jax.ShapeDtypeStruct((depth, SUBLANES, n), F32),
        compiler_params=_params("arbitrary", "arbitrary"),
        name="ada_params",
    )(cond8, ada_w, ada_b.reshape(depth, 1, n))


def _norm_mod(x, gain, shift, scale):
    ms = jnp.mean(x * x, axis=-1, keepdims=True)
    return (x * lax.rsqrt(ms + NORM_EPS) * gain) * (1.0 + scale) + shift


def _norm_mod_kernel(h_ref, g_ref, m_ref, o_ref, *, sub):
    n = _norm_mod(h_ref[...], g_ref[...], m_ref[3 * sub:3 * sub + 1, :],
                  m_ref[3 * sub + 1:3 * sub + 2, :])
    o_ref[...] = n.astype(o_ref.dtype)


def norm_mod(h, gain, mod, *, rows, n_lat, sub):
    t, d = h.shape
    bm = _pick(math.gcd(n_lat, t - n_lat) if t > n_lat else n_lat, 256)
    nlat_blocks = n_lat // bm
    return pl.pallas_call(
        functools.partial(_norm_mod_kernel, sub=sub),
        grid=(rows // bm,),
        in_specs=[
            pl.BlockSpec((bm, d), lambda i: (i, 0)),
            pl.BlockSpec((1, d), lambda i: (0, 0)),
            pl.BlockSpec((None, N_MOD, d), lambda i: (jnp.where(i >= nlat_blocks, 1, 0), 0, 0)),
        ],
        out_specs=pl.BlockSpec((bm, d), lambda i: (i, 0)),
        out_shape=jax.ShapeDtypeStruct((rows, d), BF16),
        compiler_params=_params("arbitrary"),
        name="norm_mod",
    )(h, gain.reshape(1, d), mod)


def _norm_router_kernel(h_ref, g_ref, m_ref, r_ref, o_ref, gate_ref, *, n_experts):
    n = _norm_mod(h_ref[...], g_ref[...], m_ref[3:4, :], m_ref[4:5, :])
    o_ref[...] = n.astype(o_ref.dtype)
    logits = jnp.dot(n, r_ref[...], precision=lax.Precision.HIGHEST,
                     preferred_element_type=F32)
    lane = lax.broadcasted_iota(jnp.int32, logits.shape, 1)
    neg = jnp.float32(-jnp.inf)
    lg = jnp.where(lane < n_experts, logits, neg)
    m1 = jnp.max(lg, axis=-1, keepdims=True)
    i1 = jnp.min(jnp.where(lg == m1, lane, LANES), axis=-1, keepdims=True)
    lg2 = jnp.where(lane == i1, neg, lg)
    m2 = jnp.max(lg2, axis=-1, keepdims=True)
    i2 = jnp.min(jnp.where(lg2 == m2, lane, LANES), axis=-1, keepdims=True)
    e = jnp.exp(m2 - m1)
    w1 = 1.0 / (1.0 + e)
    gate_ref[...] = jnp.where(lane == i1, w1, 0.0) + jnp.where(lane == i2, 1.0 - w1, 0.0)


def norm_router(h, gain, mod_lat, router_pad, *, rows, n_experts):
    t, d = h.shape
    bm = _pick(rows, 256)
    return pl.pallas_call(
        functools.partial(_norm_router_kernel, n_experts=n_experts),
        grid=(rows // bm,),
        in_specs=[
            pl.BlockSpec((bm, d), lambda i: (i, 0)),
            pl.BlockSpec((1, d), lambda i: (0, 0)),
            pl.BlockSpec((N_MOD, d), lambda i: (0, 0)),
            pl.BlockSpec((d, LANES), lambda i: (0, 0)),
        ],
        out_specs=[pl.BlockSpec((bm, d), lambda i: (i, 0)),
                   pl.BlockSpec((bm, LANES), lambda i: (i, 0))],
        out_shape=[jax.ShapeDtypeStruct((rows, d), BF16),
                   jax.ShapeDtypeStruct((rows, LANES), F32)],
        compiler_params=_params("arbitrary"),
        name="norm_router",
    )(h, gain.reshape(1, d), mod_lat, router_pad)


def _final_norm_kernel(h_ref, g_ref, o_ref):
    x = h_ref[...]
    ms = jnp.mean(x * x, axis=-1, keepdims=True)
    o_ref[...] = x * lax.rsqrt(ms + NORM_EPS) * g_ref[...]


def final_rms_norm(h, gain, *, rows):
    t, d = h.shape
    bm = _pick(rows, 256)
    return pl.pallas_call(
        _final_norm_kernel,
        grid=(rows // bm,),
        in_specs=[pl.BlockSpec((bm, d), lambda i: (i, 0)),
                  pl.BlockSpec((1, d), lambda i: (0, 0))],
        out_specs=pl.BlockSpec((bm, d), lambda i: (i, 0)),
        out_shape=jax.ShapeDtypeStruct((rows, d), F32),
        compiler_params=_params("arbitrary"),
        name="final_norm",
    )(h, gain.reshape(1, d))


def _rwkv_prep_kernel(h_ref, hp_ref, hn_ref, g_ref, m_ref, mix_ref, *o_refs,
                      nlat_blocks, nblocks):
    i = pl.program_id(0)
    gain, shift, scale = g_ref[...], m_ref[0:1, :], m_ref[1:2, :]
    n = _norm_mod(h_ref[...], gain, shift, scale)
    n_prev = _norm_mod(hp_ref[...], gain, shift, scale)[SUBLANES - 1:SUBLANES, :]
    n_next = _norm_mod(hn_ref[...], gain, shift, scale)[0:1, :]
    first = jnp.logical_or(i == 0, i == nlat_blocks)
    last = jnp.logical_or(i == nlat_blocks - 1, i == nblocks - 1)
    n_prev = jnp.where(first, 0.0, n_prev)
    n_next = jnp.where(last, 0.0, n_next)
    bm = n.shape[0]
    row = lax.broadcasted_iota(jnp.int32, n.shape, 0)
    prev = jnp.where(row == 0, n_prev, pltpu.roll(n, 1, 0))
    nxt = jnp.where(row == bm - 1, n_next, pltpu.roll(n, bm - 1, 0))
    dh = 0.5 * (prev + nxt) - n
    for j, o_ref in enumerate(o_refs):
        o_ref[...] = (n + dh * mix_ref[j:j + 1, :]).astype(o_ref.dtype)


def rwkv_prep(h, gain, mod, mix, *, n_lat):
    t, d = h.shape
    bm = _pick(math.gcd(n_lat, t - n_lat), 128)
    nblocks, nlat_blocks = t // bm, n_lat // bm
    r8 = bm // SUBLANES
    n8 = t // SUBLANES
    return pl.pallas_call(
        functools.partial(_rwkv_prep_kernel, nlat_blocks=nlat_blocks, nblocks=nblocks),
        grid=(nblocks,),
        in_specs=[
            pl.BlockSpec((bm, d), lambda i: (i, 0)),
            pl.BlockSpec((SUBLANES, d), lambda i: (jnp.maximum(i * r8 - 1, 0), 0)),
            pl.BlockSpec((SUBLANES, d), lambda i: (jnp.minimum((i + 1) * r8, n8 - 1), 0)),
            pl.BlockSpec((1, d), lambda i: (0, 0)),
            pl.BlockSpec((None, N_MOD, d), lambda i: (jnp.where(i >= nlat_blocks, 1, 0), 0, 0)),
            pl.BlockSpec((6, d), lambda i: (0, 0)),
        ],
        out_specs=[pl.BlockSpec((bm, d), lambda i: (i, 0))] * 6,
        out_shape=[jax.ShapeDtypeStruct((t, d), BF16)] * 6,
        compiler_params=_params("arbitrary"),
        name="rwkv_prep",
    )(h, h, h, gain.reshape(1, d), mod, mix)


def _mm_kernel(*refs, nk, n_extra, epi):
    x_ref, w_ref = refs[0], refs[1]
    extras = refs[2:2 + n_extra]
    o_ref = refs[2 + n_extra]
    i_m = pl.program_id(1)
    part = _bdot(x_ref[...], w_ref[...].astype(BF16))
    if nk == 1:
        o_ref[...] = epi(part, extras, i_m).astype(o_ref.dtype)
    else:
        acc_ref = refs[3 + n_extra]
        k = pl.program_id(2)

        @pl.when(k == 0)
        def _():
            acc_ref[...] = part

        @pl.when(k > 0)
        def _():
            acc_ref[...] += part

        @pl.when(k == nk - 1)
        def _():
            o_ref[...] = epi(acc_ref[...], extras, i_m).astype(o_ref.dtype)


def matmul(x, w, *, rows, lead=(), out_dtype=F32, epi=None, extras=(), bm=1056, bn=512,
           bk=4096, name="matmul"):
    kdim = x.shape[1]
    n = w.shape[-1]
    bm = _pick(rows, bm, 16)
    bn = _pick(n, bn, LANES)
    bk = _pick(kdim, bk, LANES)
    nk = kdim // bk
    if epi is None:
        epi = lambda acc, ex, i_m: acc
    nl = len(lead)
    in_specs = [
        pl.BlockSpec((bm, bk), lambda j, i, k: (i, k)),
        pl.BlockSpec((None,) * nl + (bk, bn), lambda j, i, k: tuple(lead) + (k, j)),
    ]
    args = [x, w]
    for arr, bshape, imap in extras:
        in_specs.append(pl.BlockSpec(bshape, imap))
        args.append(arr)
    return pl.pallas_call(
        functools.partial(_mm_kernel, nk=nk, n_extra=len(extras), epi=epi),
        grid=(n // bn, rows // bm, nk),
        in_specs=in_specs,
        out_specs=pl.BlockSpec((bm, bn), lambda j, i, k: (i, j)),
        out_shape=jax.ShapeDtypeStruct((rows, n), out_dtype),
        scratch_shapes=[pltpu.VMEM((bm, bn), F32)] if nk > 1 else [],
        compiler_params=_params("arbitrary", "arbitrary", "arbitrary"),
        name=name,
    )(*args)


def _residual_epi(n_lat, gate_idx):
    def epi(acc, ex, i_m):
        h_ref, m_ref = ex
        bm = acc.shape[0]
        row = i_m * bm + lax.broadcasted_iota(jnp.int32, (bm, 1), 0)
        gate = jnp.where(row < n_lat, m_ref[0, gate_idx:gate_idx + 1, :],
                         m_ref[1, gate_idx:gate_idx + 1, :])
        return h_ref[...] + gate * acc
    return epi


def residual_matmul(x, w, h, mod, *, rows, n_lat, gate_idx, lead=(), bk=4096, name="residual_mm"):
    n = w.shape[-1]
    bm = _pick(rows, 1056, 16)
    bn = _pick(n, 512, LANES)
    extras = [
        (h, (bm, bn), lambda j, i, k: (i, j)),
        (mod, (2, N_MOD, bn), lambda j, i, k: (0, 0, j)),
    ]
    return matmul(x, w, rows=rows, lead=lead, epi=_residual_epi(n_lat, gate_idx),
                  extras=extras, bm=bm, bn=bn, bk=bk, name=name)


def _glu_kernel(x_ref, wg_ref, wu_ref, *rest, nper, gated):
    x = x_ref[...]
    g = _bdot(x, wg_ref[...].astype(BF16))
    u = _bdot(x, wu_ref[...].astype(BF16))
    hdn = _silu(g) * u
    if gated:
        gate_ref, o_ref = rest
        e = pl.program_id(0) // nper
        gates = gate_ref[...]
        lane = lax.broadcasted_iota(jnp.int32, gates.shape, 1)
        hdn = hdn * jnp.sum(jnp.where(lane == e, gates, 0.0), axis=-1, keepdims=True)
    else:
        (o_ref,) = rest
    o_ref[...] = hdn.astype(o_ref.dtype)


def glu(x, wg, wu, *, rows, gates=None):
    n_exp, kdim, f = wg.shape
    bm = _pick(rows, 1056, 16)
    bn = _pick(f, 256, LANES)
    nper = f // bn
    in_specs = [
        pl.BlockSpec((bm, kdim), lambda j, i: (i, 0)),
        pl.BlockSpec((None, kdim, bn), lambda j, i: (j // nper, 0, j % nper)),
        pl.BlockSpec((None, kdim, bn), lambda j, i: (j // nper, 0, j % nper)),
    ]
    args = [x, wg, wu]
    if gates is not None:
        in_specs.append(pl.BlockSpec((bm, LANES), lambda j, i: (i, 0)))
        args.append(gates)
    return pl.pallas_call(
        functools.partial(_glu_kernel, nper=nper, gated=gates is not None),
        grid=(n_exp * nper, rows // bm),
        in_specs=in_specs,
        out_specs=pl.BlockSpec((bm, bn), lambda j, i: (i, j)),
        out_shape=jax.ShapeDtypeStruct((rows, n_exp * f), BF16),
        compiler_params=_params("arbitrary", "arbitrary"),
        name="glu",
    )(*args)


def _head_fold(x, n_heads):
    shift = n_heads
    while shift < LANES:
        x = x + pltpu.roll(x, shift, 1)
        shift *= 2
    return x


def _slab(j):
    return slice(j * LANES, (j + 1) * LANES)


def _rwkv_elem_kernel(t_ref, dec2_ref, iclr2_ref, dec0_ref, iclr0_ref, r_ref, k_ref, v_ref,
                      kk_ref, ka_ref, rk_ref,
                      w0_ref, w1_ref, k0_ref, k1_ref, b0_ref, b1_ref, nkk_ref, rt_ref, bonus_ref,
                      *, rd, ri, n_heads):
    bm, d = r_ref.shape
    nslab = d // LANES
    rep = LANES // n_heads
    n_keys = d // n_heads

    def put_tiled(ref, j, x):
        for i in range(rep):
            piece = x[:, i * n_heads:(i + 1) * n_heads]
            ref[:, _slab(j * rep + i)] = jnp.concatenate([piece] * rep, axis=1)

    tt = t_ref[...]
    tw = [tt[:, i * rd:(i + 1) * rd] for i in range(2)]
    ta = [tt[:, 2 * rd + i * ri:2 * rd + (i + 1) * ri] for i in range(2)]
    outs = ((w0_ref, k0_ref, b0_ref), (w1_ref, k1_ref, b1_ref))
    s_kk = jnp.zeros((bm, LANES), F32)
    s_bonus = jnp.zeros((bm, LANES), F32)
    for j in range(nslab):
        sl = _slab(j)
        k, r = k_ref[:, sl], r_ref[:, sl]
        put_tiled(rt_ref, j, r)
        kk = k * kk_ref[:, sl]
        put_tiled(nkk_ref, j, kk)
        s_kk = s_kk + kk * kk
        for dd, (w_ref, kd_ref, b_ref) in enumerate(outs):
            z = dec0_ref[dd:dd + 1, sl] + _bdot(tw[dd], dec2_ref[dd, :, sl])
            w_log = -(jnp.maximum(-z, 0.0) + jnp.log(1.0 + jnp.exp(-jnp.abs(z)))) - 0.5
            put_tiled(w_ref, j, jnp.exp(-jnp.exp(w_log)))
            a = _sigmoid(iclr0_ref[dd:dd + 1, sl] + _bdot(ta[dd], iclr2_ref[dd, :, sl]))
            k_d = k * (1.0 + (a - 1.0) * ka_ref[:, sl])
            put_tiled(kd_ref, j, k_d)
            put_tiled(b_ref, j, kk * a)
            s_bonus = s_bonus + r * k_d * rk_ref[:, sl]
    inv = lax.rsqrt(jnp.maximum(_head_fold(s_kk, n_heads), 1e-24))
    s_bonus = _head_fold(s_bonus, n_heads)
    for c in range(n_keys):
        sl = _slab(c)
        nkk_ref[:, sl] = -nkk_ref[:, sl] * inv
        b0_ref[:, sl] = b0_ref[:, sl] * inv
        b1_ref[:, sl] = b1_ref[:, sl] * inv
    for j in range(nslab):
        bonus_ref[:, _slab(j)] = s_bonus * v_ref[:, _slab(j)]


def rwkv_elem(tt, dec2, iclr2, dec0, iclr0, r, k, v, kk, ka, rk, *, rd, ri, n_heads):
    t, d = r.shape
    bm = _pick(t, 32, 16)
    tw = tt.shape[1]
    dt = (d // n_heads) * LANES
    tile = pl.BlockSpec((bm, d), lambda i: (i, 0))
    tiled = pl.BlockSpec((bm, dt), lambda i: (i, 0))
    row = pl.BlockSpec((1, d), lambda i: (0, 0))
    return pl.pallas_call(
        functools.partial(_rwkv_elem_kernel, rd=rd, ri=ri, n_heads=n_heads),
        grid=(t // bm,),
        in_specs=[
            pl.BlockSpec((bm, tw), lambda i: (i, 0)),
            pl.BlockSpec((2, rd, d), lambda i: (0, 0, 0)),
            pl.BlockSpec((2, ri, d), lambda i: (0, 0, 0)),
            pl.BlockSpec((2, d), lambda i: (0, 0)),
            pl.BlockSpec((2, d), lambda i: (0, 0)),
            tile, tile, tile, row, row, row,
        ],
        out_specs=[tiled] * 8 + [tile],
        out_shape=[jax.ShapeDtypeStruct((t, dt), F32)] * 8 + [jax.ShapeDtypeStruct((t, d), F32)],
        compiler_params=_params("arbitrary"),
        name="rwkv_elem",
    )(tt, dec2, iclr2, dec0, iclr0, r, k, v, kk, ka, rk)


def _scan_kernel(rf, wf, kf, vf, af, bf, rb, wb, kb, vb, ab, bb, yf_ref, yb_ref,
                 sf_ref, sb_ref, *, tb, n):
    @pl.when(pl.program_id(0) == 0)
    def _():
        sf_ref[...] = jnp.zeros_like(sf_ref)
        sb_ref[...] = jnp.zeros_like(sb_ref)

    nvg = sf_ref.shape[1]
    rows = min(SUBLANES, nvg)
    nvb = nvg // rows

    def vslice(i):
        return pl.ds(i * rows, rows)

    def bcast(ref, t, kc):
        return jnp.broadcast_to(ref[t, pl.ds(kc, 1), :], (rows, LANES))

    def zeros():
        return [jnp.zeros((rows, LANES), F32) for _ in range(nvb)]

    def state_dot_a(t, a_ref, s_ref):
        sa = zeros()
        for kc in range(n):
            a_k = bcast(a_ref, t, kc)
            for i in range(nvb):
                sa[i] = sa[i] + s_ref[kc, vslice(i), :] * a_k
        return tuple(sa)

    def one_direction(t, t_next, r_ref, w_ref, k_ref, v_ref, a_ref, b_ref, y_ref, s_ref, sa):
        vv = [v_ref[t, vslice(i), :] for i in range(nvb)]
        y, sa_next = zeros(), zeros()
        for kc in range(n):
            w_k, b_k, k_k = bcast(w_ref, t, kc), bcast(b_ref, t, kc), bcast(k_ref, t, kc)
            r_k, a_k = bcast(r_ref, t, kc), bcast(a_ref, t_next, kc)
            for i in range(nvb):
                s_new = s_ref[kc, vslice(i), :] * w_k + sa[i] * b_k + vv[i] * k_k
                s_ref[kc, vslice(i), :] = s_new
                y[i] = y[i] + s_new * r_k
                sa_next[i] = sa_next[i] + s_new * a_k
        for i in range(nvb):
            y_ref[t, vslice(i), :] = y[i]
        return tuple(sa_next)

    def step(t, carry):
        sa_f, sa_b = carry
        sa_f = one_direction(t, jnp.minimum(t + 1, tb - 1), rf, wf, kf, vf, af, bf, yf_ref,
                             sf_ref, sa_f)
        u = tb - 1 - t
        sa_b = one_direction(u, jnp.maximum(u - 1, 0), rb, wb, kb, vb, ab, bb, yb_ref,
                             sb_ref, sa_b)
        return sa_f, sa_b

    init = (state_dot_a(0, af, sf_ref), state_dot_a(tb - 1, ab, sb_ref))
    lax.fori_loop(0, tb, step, init)


def rwkv_scan(r, w0, w1, k0, k1, v, a, b0, b1, *, n_lat):
    t, d = v.shape
    n = r.shape[1] // LANES
    nvg = d // LANES
    tb = _pick(math.gcd(n_lat, t - n_lat), 32, 1)
    nb = t // tb
    first = n_lat // tb
    fwd = lambda i: ((i + first) % nb, 0, 0)
    bwd = lambda i: (nb - 1 - i, 0, 0)
    kview = lambda x: x.reshape(t, n, LANES)
    vview = lambda x: x.reshape(t, nvg, LANES)

    def specs(imap):
        ks = pl.BlockSpec((tb, n, LANES), imap)
        vs = pl.BlockSpec((tb, nvg, LANES), imap)
        return [ks, ks, ks, vs, ks, ks], vs

    in_f, out_f = specs(fwd)
    in_b, out_b = specs(bwd)
    y_f, y_b = pl.pallas_call(
        functools.partial(_scan_kernel, tb=tb, n=n),
        grid=(nb,),
        in_specs=in_f + in_b,
        out_specs=[out_f, out_b],
        out_shape=[jax.ShapeDtypeStruct((t, nvg, LANES), F32)] * 2,
        scratch_shapes=[pltpu.VMEM((n, nvg, LANES), F32)] * 2,
        compiler_params=_params("arbitrary"),
        name="rwkv_scan",
    )(kview(r), kview(w0), kview(k0), vview(v), kview(a), kview(b0),
      kview(r), kview(w1), kview(k1), vview(v), kview(a), kview(b1))
    return y_f.reshape(t, d), y_b.reshape(t, d)


def _rwkv_post_kernel(yf_ref, yb_ref, bonus_ref, g_ref, lw_ref, lb_ref, o_ref, *, n_heads):
    bm, d = yf_ref.shape
    nslab = d // LANES
    inv_n = 1.0 / (d // n_heads)
    s1 = jnp.zeros((bm, LANES), F32)
    for j in range(nslab):
        s1 = s1 + (yf_ref[:, _slab(j)] + yb_ref[:, _slab(j)])
    mu = _head_fold(s1, n_heads) * inv_n
    s2 = jnp.zeros((bm, LANES), F32)
    for j in range(nslab):
        yc = yf_ref[:, _slab(j)] + yb_ref[:, _slab(j)] - mu
        s2 = s2 + yc * yc
    rstd = lax.rsqrt(_head_fold(s2, n_heads) * inv_n + LNX_EPS)
    for j in range(nslab):
        sl = _slab(j)
        yn = (yf_ref[:, sl] + yb_ref[:, sl] - mu) * rstd
        out = yn * lw_ref[:, sl] + lb_ref[:, sl] + bonus_ref[:, sl]
        o_ref[:, sl] = (out * g_ref[:, sl].astype(F32)).astype(o_ref.dtype)


def rwkv_post(yf, yb, bonus, g, lnx_w, lnx_b, *, n_heads):
    t, d = yf.shape
    bm = _pick(t, 128, 16)
    tile = pl.BlockSpec((bm, d), lambda i: (i, 0))
    row = pl.BlockSpec((1, d), lambda i: (0, 0))
    return pl.pallas_call(
        functools.partial(_rwkv_post_kernel, n_heads=n_heads),
        grid=(t // bm,),
        in_specs=[tile, tile, tile, tile, row, row],
        out_specs=tile,
        out_shape=jax.ShapeDtypeStruct((t, d), BF16),
        compiler_params=_params("arbitrary"),
        name="rwkv_post",
    )(yf, yb, bonus, g, lnx_w.reshape(1, d), lnx_b.reshape(1, d))


def _qk_epi(scale):
    def epi(acc, ex, i_m):
        gain_ref, cos_ref, sin_ref = ex
        cos, sin = cos_ref[...], sin_ref[...]
        gain = gain_ref[...] * scale
        lane = lax.broadcasted_iota(jnp.int32, cos.shape, 1)
        even = (lane % 2) == 0
        outs = []
        for j in range(acc.shape[1] // HEAD_DIM):
            q = acc[:, j * HEAD_DIM:(j + 1) * HEAD_DIM]
            ms = jnp.mean(q * q, axis=-1, keepdims=True)
            q = q * lax.rsqrt(ms + NORM_EPS) * gain
            partner = jnp.where(even, pltpu.roll(q, HEAD_DIM - 1, 1), pltpu.roll(q, 1, 1))
            outs.append(q * cos + partner * sin)
        return jnp.concatenate(outs, axis=1)
    return epi


def qk_matmul(x, w, gain, cos_t, sin_t, *, rows, scale, lead=(), name="qk_mm"):
    n = w.shape[-1]
    bm = _pick(rows, 1024, 16)
    bn = _pick(n, 512, LANES)
    extras = [
        (gain.reshape(1, HEAD_DIM), (1, HEAD_DIM), lambda j, i, k: (0, 0)),
        (cos_t, (bm, HEAD_DIM), lambda j, i, k: (i, 0)),
        (sin_t, (bm, HEAD_DIM), lambda j, i, k: (i, 0)),
    ]
    return matmul(x, w, rows=rows, lead=lead, out_dtype=BF16, epi=_qk_epi(scale),
                  extras=extras, bm=bm, bn=bn, name=name)


def _attn_kernel(q_ref, k_ref, v_ref, o_ref, *, groups, bk, nchunks):
    bq = q_ref.shape[0]
    q = jnp.concatenate([q_ref[:, g * HEAD_DIM:(g + 1) * HEAD_DIM] for g in range(groups)], axis=0)
    rows = groups * bq

    def body(c, carry):
        m, l, acc = carry
        start = pl.multiple_of(c * bk, bk)
        kc = k_ref[pl.ds(start, bk), :]
        vc = v_ref[pl.ds(start, bk), :]
        s = lax.dot_general(q, kc, (((1,), (1,)), ((), ())), preferred_element_type=F32)
        m_new = jnp.maximum(m, jnp.max(s, axis=1, keepdims=True))
        alpha = jnp.exp(m - m_new)
        p = jnp.exp(s - m_new)
        l = alpha * l + jnp.sum(p, axis=1, keepdims=True)
        acc = alpha * acc + _bdot(p.astype(BF16), vc)
        return m_new, l, acc

    init = (jnp.full((rows, 1), -jnp.inf, F32), jnp.zeros((rows, 1), F32),
            jnp.zeros((rows, HEAD_DIM), F32))
    _, l, acc = lax.fori_loop(0, nchunks, body, init)
    out = acc / l
    for g in range(groups):
        o_ref[:, g * HEAD_DIM:(g + 1) * HEAD_DIM] = out[g * bq:(g + 1) * bq].astype(o_ref.dtype)


def attention(q, k, v, *, groups):
    s, dq = q.shape
    tk, dkv = k.shape
    n_kv = dkv // HEAD_DIM
    bq = _pick(s, 256, 16)
    bk = _pick(tk, 768, 256) if tk % 256 == 0 else tk
    return pl.pallas_call(
        functools.partial(_attn_kernel, groups=groups, bk=bk, nchunks=tk // bk),
        grid=(n_kv, s // bq),
        in_specs=[
            pl.BlockSpec((bq, groups * HEAD_DIM), lambda h, i: (i, h)),
            pl.BlockSpec((tk, HEAD_DIM), lambda h, i: (0, h)),
            pl.BlockSpec((tk, HEAD_DIM), lambda h, i: (0, h)),
        ],
        out_specs=pl.BlockSpec((bq, groups * HEAD_DIM), lambda h, i: (i, h)),
        out_shape=jax.ShapeDtypeStruct((s, dq), BF16),
        compiler_params=_params("arbitrary", "arbitrary"),
        name="attention",
    )(q, k, v)


def _head_minor_cols(w, n_heads):
    lead = w.shape[:-1]
    n = w.shape[-1] // n_heads
    return jnp.swapaxes(w.reshape(lead + (n_heads, n)), -1, -2).reshape(lead + (n_heads * n,))


def _head_minor_rows(w, n_heads):
    rows, cols = w.shape
    return jnp.swapaxes(w.reshape(n_heads, rows // n_heads, cols), 0, 1).reshape(rows, cols)


def _pad_cols(w, mult):
    pad = (-w.shape[-1]) % mult
    return jnp.pad(w, [(0, 0)] * (w.ndim - 1) + [(0, pad)]) if pad else w


def kernel(x, c, ctx, c_ctx, ada_w, ada_b, norm_gains, rwkv_mix, rwkv_wr, rwkv_wk, rwkv_wv, rwkv_wo, rwkv_dec0, rwkv_dec1, rwkv_dec2, rwkv_iclr0, rwkv_iclr1, rwkv_iclr2, rwkv_gate1, rwkv_gate2, rwkv_kk, rwkv_ka, rwkv_rk, rwkv_lnx_w, rwkv_lnx_b, ffn_w_gate, ffn_w_up, ffn_w_down, attn_wq, attn_wk, attn_wv, attn_wo, attn_q_gain, attn_k_gain, moe_router, moe_w_gate, moe_w_up, moe_w_down, final_norm):
    batch, n_lat, d = x.shape
    n_ctx = ctx.shape[1]
    depth = ada_w.shape[0]
    assert batch == 1 and depth == 2 and rwkv_dec0.shape[1] == 2
    t = n_lat + n_ctx
    n_experts = moe_router.shape[-1]
    assert n_experts <= LANES and TOP_K == 2

    h = jnp.concatenate([x[0], ctx[0]], axis=0)

    cond8 = jnp.zeros((SUBLANES, d), F32).at[0].set(c[0]).at[1].set(c_ctx)
    mod = ada_params(cond8, ada_w, ada_b)[:, :2, :].reshape(depth, 2, N_MOD, d)

    xr, xw, xk, xv, xa, xg = rwkv_prep(h, norm_gains[0, 0], mod[0], rwkv_mix[0], n_lat=n_lat)
    nh = d // RWKV_HEAD
    hm = functools.partial(_head_minor_cols, n_heads=nh)
    r = matmul(xr, hm(rwkv_wr[0]).astype(BF16), rows=t, name="rwkv_r")
    k = matmul(xk, hm(rwkv_wk[0]).astype(BF16), rows=t, name="rwkv_k")
    v = matmul(xv, hm(rwkv_wv[0]).astype(BF16), rows=t, name="rwkv_v")

    rd, ri = rwkv_dec1.shape[-1], rwkv_iclr1.shape[-1]
    dec1 = jnp.concatenate([rwkv_dec1[0, 0], rwkv_dec1[0, 1]], axis=1)
    iclr1 = jnp.concatenate([rwkv_iclr1[0, 0], rwkv_iclr1[0, 1]], axis=1)
    gate1 = _pad_cols(rwkv_gate1[0], LANES)
    rg = gate1.shape[1]
    gate2 = jnp.pad(hm(rwkv_gate2[0]), ((0, rg - rwkv_gate2.shape[1]), (0, 0)))
    tw = matmul(xw, dec1, rows=t, out_dtype=BF16, epi=lambda a, e, i: jnp.tanh(a), name="lora_w")
    ta = matmul(xa, iclr1, rows=t, out_dtype=BF16, name="lora_a")
    tg = matmul(xg, gate1, rows=t, out_dtype=BF16, epi=lambda a, e, i: _sigmoid(a), name="lora_g")
    g = matmul(tg, gate2, rows=t, out_dtype=BF16, name="rwkv_gate")

    tt = jnp.concatenate([tw, ta], axis=1)
    w0, w1, k0, k1, b0, b1, nkk, rt, bonus = rwkv_elem(
        tt, hm(rwkv_dec2[0]).astype(BF16), hm(rwkv_iclr2[0]).astype(BF16), hm(rwkv_dec0[0]),
        hm(rwkv_iclr0[0]), r, k, v, hm(rwkv_kk[0]).reshape(1, d), hm(rwkv_ka[0]).reshape(1, d),
        hm(rwkv_rk[0].reshape(d)).reshape(1, d), rd=rd, ri=ri, n_heads=nh)

    y_f, y_b = rwkv_scan(rt, w0, w1, k0, k1, v, nkk, b0, b1, n_lat=n_lat)
    yo = rwkv_post(y_f, y_b, bonus, g, hm(rwkv_lnx_w[0]), hm(rwkv_lnx_b[0]), n_heads=nh)
    h = residual_matmul(yo, _head_minor_rows(rwkv_wo[0], nh).astype(BF16), h, mod[0], rows=t,
                        n_lat=n_lat, gate_idx=2, name="rwkv_out")

    nf = norm_mod(h, norm_gains[0, 1], mod[0], rows=t, n_lat=n_lat, sub=1)
    hid = glu(nf, ffn_w_gate, ffn_w_up, rows=t)
    h = residual_matmul(hid, ffn_w_down.astype(BF16), h, mod[0], rows=t, n_lat=n_lat, gate_idx=5,
                        lead=(0,), bk=_pick(ffn_w_down.shape[1], 5504, LANES), name="ffn_down")

    na = norm_mod(h, norm_gains[1, 0], mod[1], rows=t, n_lat=n_lat, sub=0)
    pairs = HEAD_DIM // 4
    inv_freq = ROPE_THETA ** (-jnp.arange(pairs, dtype=F32) / pairs)
    pos = jnp.arange(n_lat)
    ang = jnp.concatenate([(pos // GRID_W).astype(F32)[:, None] * inv_freq,
                           (pos % GRID_W).astype(F32)[:, None] * inv_freq], axis=-1)
    sign = jnp.tile(jnp.asarray([-1.0, 1.0], F32), HEAD_DIM // 2)
    cos_t = jnp.concatenate([jnp.repeat(jnp.cos(ang), 2, axis=-1),
                             jnp.ones((n_ctx, HEAD_DIM), F32)], axis=0)
    sin_t = jnp.concatenate([jnp.repeat(jnp.sin(ang), 2, axis=-1) * sign,
                             jnp.zeros((n_ctx, HEAD_DIM), F32)], axis=0)
    q = qk_matmul(na, attn_wq, attn_q_gain[0], cos_t, sin_t, rows=n_lat,
                  scale=HEAD_DIM ** -0.5, lead=(0,), name="attn_q")
    kx = qk_matmul(na, attn_wk, attn_k_gain[0], cos_t, sin_t, rows=t, scale=1.0, lead=(0,),
                   name="attn_k")
    vx = matmul(na, attn_wv, rows=t, lead=(0,), out_dtype=BF16, name="attn_v")
    groups = attn_wq.shape[-1] // attn_wk.shape[-1]
    o = attention(q, kx, vx, groups=groups)
    h = residual_matmul(o, attn_wo, h, mod[1], rows=n_lat, n_lat=n_lat, gate_idx=2, lead=(0,),
                        name="attn_out")

    nm, gates = norm_router(h, norm_gains[1, 1], mod[1, 0], _pad_cols(moe_router[0], LANES),
                            rows=n_lat, n_experts=n_experts)
    hid = glu(nm, moe_w_gate[0], moe_w_up[0], rows=n_lat, gates=gates)
    wd = moe_w_down[0].reshape(-1, d)
    h = residual_matmul(hid, wd, h, mod[1], rows=n_lat, n_lat=n_lat, gate_idx=5, name="moe_down")

    return final_rms_norm(h, final_norm, rows=n_lat)[None]
```

```python
import functools
import math

import numpy as np
import jax
import jax.numpy as jnp
from jax import lax
from jax.experimental import pallas as pl
from jax.experimental.pallas import tpu as pltpu

F32 = jnp.float32
BF16 = jnp.bfloat16

N_MOD = 6
NORM_EPS = 1e-6
LNX_EPS = 64e-5
RWKV_HEAD = 64
HEAD_DIM = 128
GRID_W = 64
ROPE_THETA = 10000.0
TOP_K = 2
LANES = 128
SUBLANES = 8
VMEM_LIMIT = 56 * 1024 * 1024


def _pick(n, target, mult=SUBLANES):
    best = None
    for b in range(mult, min(n, target) + 1, mult):
        if n % b == 0:
            best = b
    return best if best is not None else n


def _params(*sem):
    return pltpu.CompilerParams(dimension_semantics=sem, vmem_limit_bytes=VMEM_LIMIT)


def _sigmoid(x):
    return 1.0 / (1.0 + jnp.exp(-x))


def _silu(x):
    return x * _sigmoid(x)


def _bdot(a, b):
    return jnp.dot(a, b, preferred_element_type=F32)


def _ada_kernel(c_ref, w_ref, b_ref, o_ref):
    s = _silu(c_ref[...]).astype(BF16)
    o_ref[...] = _bdot(s, w_ref[...].astype(BF16)) + b_ref[...]


def ada_params(cond8, ada_w, ada_b):
    depth, d, n = ada_w.shape
    bn = _pick(n, 512, LANES)
    return pl.pallas_call(
        _ada_kernel,
        grid=(depth, n // bn),
        in_specs=[
            pl.BlockSpec((SUBLANES, d), lambda l, j: (0, 0)),
            pl.BlockSpec((None, d, bn), lambda l, j: (l, 0, j)),
            pl.BlockSpec((None, 1, bn), lambda l, j: (l, 0, j)),
        ],
        out_specs=pl.BlockSpec((None, SUBLANES, bn), lambda l, j: (l, 0, j)),
        out_shape=jax.ShapeDtypeStruct((depth, SUBLANES, n), F32),
        compiler_params=_params("arbitrary", "arbitrary"),
        name="ada_params",
    )(cond8, ada_w, ada_b.reshape(depth, 1, n))


def _norm_mod(x, gain, shift, scale):
    ms = jnp.mean(x * x, axis=-1, keepdims=True)
    return (x * lax.rsqrt(ms + NORM_EPS) * gain) * (1.0 + scale) + shift


def _norm_mod_kernel(h_ref, g_ref, m_ref, o_ref, *, sub):
    n = _norm_mod(h_ref[...], g_ref[...], m_ref[3 * sub:3 * sub + 1, :],
                  m_ref[3 * sub + 1:3 * sub + 2, :])
    o_ref[...] = n.astype(o_ref.dtype)


def norm_mod(h, gain, mod, *, rows, n_lat, sub):
    t, d = h.shape
    bm = _pick(math.gcd(n_lat, t - n_lat) if t > n_lat else n_lat, 256)
    nlat_blocks = n_lat // bm
    return pl.pallas_call(
        functools.partial(_norm_mod_kernel, sub=sub),
        grid=(rows // bm,),
        in_specs=[
            pl.BlockSpec((bm, d), lambda i: (i, 0)),
            pl.BlockSpec((1, d), lambda i: (0, 0)),
            pl.BlockSpec((None, N_MOD, d), lambda i: (jnp.where(i >= nlat_blocks, 1, 0), 0, 0)),
        ],
        out_specs=pl.BlockSpec((bm, d), lambda i: (i, 0)),
        out_shape=jax.ShapeDtypeStruct((rows, d), BF16),
        compiler_params=_params("arbitrary"),
        name="norm_mod",
    )(h, gain.reshape(1, d), mod)


def _norm_router_kernel(h_ref, g_ref, m_ref, r_ref, o_ref, idx_ref, wgt_ref, *, n_experts):
    n = _norm_mod(h_ref[...], g_ref[...], m_ref[3:4, :], m_ref[4:5, :])
    o_ref[...] = n.astype(o_ref.dtype)
    logits = jnp.dot(n, r_ref[...], precision=lax.Precision.HIGHEST,
                     preferred_element_type=F32)
    lane = lax.broadcasted_iota(jnp.int32, logits.shape, 1)
    neg = jnp.float32(-jnp.inf)
    lg = jnp.where(lane < n_experts, logits, neg)
    m1 = jnp.max(lg, axis=-1, keepdims=True)
    i1 = jnp.min(jnp.where(lg == m1, lane, LANES), axis=-1, keepdims=True)
    lg2 = jnp.where(lane == i1, neg, lg)
    m2 = jnp.max(lg2, axis=-1, keepdims=True)
    i2 = jnp.min(jnp.where(lg2 == m2, lane, LANES), axis=-1, keepdims=True)
    e = jnp.exp(m2 - m1)
    w1 = 1.0 / (1.0 + e)
    idx_ref[...] = jnp.where(lane == 0, i1, jnp.where(lane == 1, i2, 0))
    wgt_ref[...] = jnp.where(lane == 0, w1, jnp.where(lane == 1, 1.0 - w1, 0.0))


def norm_router(h, gain, mod_lat, router_pad, *, rows, n_experts):
    t, d = h.shape
    bm = _pick(rows, 256)
    lane_blk = pl.BlockSpec((bm, LANES), lambda i: (i, 0))
    return pl.pallas_call(
        functools.partial(_norm_router_kernel, n_experts=n_experts),
        grid=(rows // bm,),
        in_specs=[
            pl.BlockSpec((bm, d), lambda i: (i, 0)),
            pl.BlockSpec((1, d), lambda i: (0, 0)),
            pl.BlockSpec((N_MOD, d), lambda i: (0, 0)),
            pl.BlockSpec((d, LANES), lambda i: (0, 0)),
        ],
        out_specs=[pl.BlockSpec((bm, d), lambda i: (i, 0)), lane_blk, lane_blk],
        out_shape=[jax.ShapeDtypeStruct((rows, d), BF16),
                   jax.ShapeDtypeStruct((rows, LANES), jnp.int32),
                   jax.ShapeDtypeStruct((rows, LANES), F32)],
        compiler_params=_params("arbitrary"),
        name="norm_router",
    )(h, gain.reshape(1, d), mod_lat, router_pad)


def _final_norm_kernel(h_ref, g_ref, o_ref):
    x = h_ref[...]
    ms = jnp.mean(x * x, axis=-1, keepdims=True)
    o_ref[...] = x * lax.rsqrt(ms + NORM_EPS) * g_ref[...]


def final_rms_norm(h, gain, *, rows):
    t, d = h.shape
    bm = _pick(rows, 256)
    return pl.pallas_call(
        _final_norm_kernel,
        grid=(rows // bm,),
        in_specs=[pl.BlockSpec((bm, d), lambda i: (i, 0)),
                  pl.BlockSpec((1, d), lambda i: (0, 0))],
        out_specs=pl.BlockSpec((bm, d), lambda i: (i, 0)),
        out_shape=jax.ShapeDtypeStruct((rows, d), F32),
        compiler_params=_params("arbitrary"),
        name="final_norm",
    )(h, gain.reshape(1, d))


def _rwkv_prep_kernel(h_ref, hp_ref, hn_ref, g_ref, m_ref, mix_ref, *o_refs,
                      nlat_blocks, nblocks):
    i = pl.program_id(0)
    gain, shift, scale = g_ref[...], m_ref[0:1, :], m_ref[1:2, :]
    n = _norm_mod(h_ref[...], gain, shift, scale)
    n_prev = _norm_mod(hp_ref[...], gain, shift, scale)[SUBLANES - 1:SUBLANES, :]
    n_next = _norm_mod(hn_ref[...], gain, shift, scale)[0:1, :]
    first = jnp.logical_or(i == 0, i == nlat_blocks)
    last = jnp.logical_or(i == nlat_blocks - 1, i == nblocks - 1)
    n_prev = jnp.where(first, 0.0, n_prev)
    n_next = jnp.where(last, 0.0, n_next)
    bm = n.shape[0]
    row = lax.broadcasted_iota(jnp.int32, n.shape, 0)
    prev = jnp.where(row == 0, n_prev, pltpu.roll(n, 1, 0))
    nxt = jnp.where(row == bm - 1, n_next, pltpu.roll(n, bm - 1, 0))
    dh = 0.5 * (prev + nxt) - n
    for j, o_ref in enumerate(o_refs):
        o_ref[...] = (n + dh * mix_ref[j:j + 1, :]).astype(o_ref.dtype)


def rwkv_prep(h, gain, mod, mix, *, n_lat):
    t, d = h.shape
    bm = _pick(math.gcd(n_lat, t - n_lat), 128)
    nblocks, nlat_blocks = t // bm, n_lat // bm
    r8 = bm // SUBLANES
    n8 = t // SUBLANES
    return pl.pallas_call(
        functools.partial(_rwkv_prep_kernel, nlat_blocks=nlat_blocks, nblocks=nblocks),
        grid=(nblocks,),
        in_specs=[
            pl.BlockSpec((bm, d), lambda i: (i, 0)),
            pl.BlockSpec((SUBLANES, d), lambda i: (jnp.maximum(i * r8 - 1, 0), 0)),
            pl.BlockSpec((SUBLANES, d), lambda i: (jnp.minimum((i + 1) * r8, n8 - 1), 0)),
            pl.BlockSpec((1, d), lambda i: (0, 0)),
            pl.BlockSpec((None, N_MOD, d), lambda i: (jnp.where(i >= nlat_blocks, 1, 0), 0, 0)),
            pl.BlockSpec((6, d), lambda i: (0, 0)),
        ],
        out_specs=[pl.BlockSpec((bm, d), lambda i: (i, 0))] * 6,
        out_shape=[jax.ShapeDtypeStruct((t, d), BF16)] * 6,
        compiler_params=_params("arbitrary"),
        name="rwkv_prep",
    )(h, h, h, gain.reshape(1, d), mod, mix)


def _mm_kernel(*refs, nk, n_extra, epi, grouped):
    if grouped:
        refs = refs[1:]
    x_ref, w_ref = refs[0], refs[1]
    extras = refs[2:2 + n_extra]
    o_ref = refs[2 + n_extra]
    i_m = pl.program_id(1)
    part = _bdot(x_ref[...], w_ref[...].astype(BF16))
    if nk == 1:
        o_ref[...] = epi(part, extras, i_m).astype(o_ref.dtype)
    else:
        acc_ref = refs[3 + n_extra]
        k = pl.program_id(2)

        @pl.when(k == 0)
        def _():
            acc_ref[...] = part

        @pl.when(k > 0)
        def _():
            acc_ref[...] += part

        @pl.when(k == nk - 1)
        def _():
            o_ref[...] = epi(acc_ref[...], extras, i_m).astype(o_ref.dtype)


def matmul(x, w, *, rows, lead=(), block_group=None, out_dtype=F32, epi=None, extras=(),
           bm=1056, bn=512, bk=4096, name="matmul"):
    kdim = x.shape[1]
    n = w.shape[-1]
    bm = _pick(rows, bm, 16)
    bn = _pick(n, bn, LANES)
    bk = _pick(kdim, bk, LANES)
    nk = kdim // bk
    if epi is None:
        epi = lambda acc, ex, i_m: acc
    grouped = block_group is not None
    if grouped:
        w_spec = pl.BlockSpec((None, bk, bn), lambda j, i, k, grp: (grp[i], k, j))
    else:
        w_spec = pl.BlockSpec((None,) * len(lead) + (bk, bn),
                              lambda j, i, k: tuple(lead) + (k, j))
    in_specs = [pl.BlockSpec((bm, bk), lambda j, i, k, *_: (i, k)), w_spec]
    args = [x, w]
    for arr, bshape, imap in extras:
        in_specs.append(pl.BlockSpec(bshape, imap))
        args.append(arr)
    if grouped:
        args = [block_group] + args
    grid_spec = pltpu.PrefetchScalarGridSpec(
        num_scalar_prefetch=1 if grouped else 0,
        grid=(n // bn, rows // bm, nk),
        in_specs=in_specs,
        out_specs=pl.BlockSpec((bm, bn), lambda j, i, k, *_: (i, j)),
        scratch_shapes=[pltpu.VMEM((bm, bn), F32)] if nk > 1 else [],
    )
    return pl.pallas_call(
        functools.partial(_mm_kernel, nk=nk, n_extra=len(extras), epi=epi, grouped=grouped),
        grid_spec=grid_spec,
        out_shape=jax.ShapeDtypeStruct((rows, n), out_dtype),
        compiler_params=_params("arbitrary", "arbitrary", "arbitrary"),
        name=name,
    )(*args)


def _residual_epi(n_lat, gate_idx):
    def epi(acc, ex, i_m):
        h_ref, m_ref = ex
        bm = acc.shape[0]
        row = i_m * bm + lax.broadcasted_iota(jnp.int32, (bm, 1), 0)
        gate = jnp.where(row < n_lat, m_ref[0, gate_idx:gate_idx + 1, :],
                         m_ref[1, gate_idx:gate_idx + 1, :])
        return h_ref[...] + gate * acc
    return epi


def residual_matmul(x, w, h, mod, *, rows, n_lat, gate_idx, lead=(), bk=4096, name="residual_mm"):
    n = w.shape[-1]
    bm = _pick(rows, 1056, 16)
    bn = _pick(n, 512, LANES)
    extras = [
        (h, (bm, bn), lambda j, i, k: (i, j)),
        (mod, (2, N_MOD, bn), lambda j, i, k: (0, 0, j)),
    ]
    return matmul(x, w, rows=rows, lead=lead, epi=_residual_epi(n_lat, gate_idx),
                  extras=extras, bm=bm, bn=bn, bk=bk, name=name)


def _glu_kernel(*refs, grouped):
    if grouped:
        _, x_ref, wg_ref, wu_ref, scale_ref, o_ref = refs
    else:
        x_ref, wg_ref, wu_ref, o_ref = refs
    x = x_ref[...]
    g = _bdot(x, wg_ref[...].astype(BF16))
    u = _bdot(x, wu_ref[...].astype(BF16))
    hdn = _silu(g) * u
    if grouped:
        hdn = hdn * scale_ref[...]
    o_ref[...] = hdn.astype(o_ref.dtype)


def glu(x, wg, wu, *, rows, bm=1056, block_expert=None, row_scale=None):
    n_exp, kdim, f = wg.shape
    grouped = block_expert is not None
    assert grouped or n_exp == 1
    bm = _pick(rows, bm, 16)
    bn = _pick(f, 256, LANES)
    w_map = (lambda j, i, be: (be[i], 0, j)) if grouped else (lambda j, i: (0, 0, j))
    in_specs = [
        pl.BlockSpec((bm, kdim), lambda j, i, *_: (i, 0)),
        pl.BlockSpec((None, kdim, bn), w_map),
        pl.BlockSpec((None, kdim, bn), w_map),
    ]
    args = [x, wg, wu]
    if grouped:
        in_specs.append(pl.BlockSpec((bm, 1), lambda j, i, *_: (i, 0)))
        args = [block_expert] + args + [row_scale]
    grid_spec = pltpu.PrefetchScalarGridSpec(
        num_scalar_prefetch=1 if grouped else 0,
        grid=(f // bn, rows // bm),
        in_specs=in_specs,
        out_specs=pl.BlockSpec((bm, bn), lambda j, i, *_: (i, j)),
    )
    return pl.pallas_call(
        functools.partial(_glu_kernel, grouped=grouped),
        grid_spec=grid_spec,
        out_shape=jax.ShapeDtypeStruct((rows, f), BF16),
        compiler_params=_params("arbitrary", "arbitrary"),
        name="glu",
    )(*args)


def _head_fold(x, n_heads):
    shift = n_heads
    while shift < LANES:
        x = x + pltpu.roll(x, shift, 1)
        shift *= 2
    return x


def _slab(j):
    return slice(j * LANES, (j + 1) * LANES)


def _rwkv_elem_kernel(t_ref, dec2_ref, iclr2_ref, dec0_ref, iclr0_ref, r_ref, k_ref, v_ref,
                      kk_ref, ka_ref, rk_ref,
                      w0_ref, w1_ref, k0_ref, k1_ref, b0_ref, b1_ref, nkk_ref, rt_ref, bonus_ref,
                      *, rd, ri, n_heads):
    bm, d = r_ref.shape
    nslab = d // LANES
    rep = LANES // n_heads
    n_keys = d // n_heads

    def put_tiled(ref, j, x):
        for i in range(rep):
            piece = x[:, i * n_heads:(i + 1) * n_heads]
            ref[:, _slab(j * rep + i)] = jnp.concatenate([piece] * rep, axis=1)

    tt = t_ref[...]
    tw = [tt[:, i * rd:(i + 1) * rd] for i in range(2)]
    ta = [tt[:, 2 * rd + i * ri:2 * rd + (i + 1) * ri] for i in range(2)]
    outs = ((w0_ref, k0_ref, b0_ref), (w1_ref, k1_ref, b1_ref))
    s_kk = jnp.zeros((bm, LANES), F32)
    s_bonus = jnp.zeros((bm, LANES), F32)
    for j in range(nslab):
        sl = _slab(j)
        k, r = k_ref[:, sl], r_ref[:, sl]
        put_tiled(rt_ref, j, r)
        kk = k * kk_ref[:, sl]
        put_tiled(nkk_ref, j, kk)
        s_kk = s_kk + kk * kk
        for dd, (w_ref, kd_ref, b_ref) in enumerate(outs):
            z = dec0_ref[dd:dd + 1, sl] + _bdot(tw[dd], dec2_ref[dd, :, sl])
            w_log = -(jnp.maximum(-z, 0.0) + jnp.log(1.0 + jnp.exp(-jnp.abs(z)))) - 0.5
            put_tiled(w_ref, j, jnp.exp(-jnp.exp(w_log)))
            a = _sigmoid(iclr0_ref[dd:dd + 1, sl] + _bdot(ta[dd], iclr2_ref[dd, :, sl]))
            k_d = k * (1.0 + (a - 1.0) * ka_ref[:, sl])
            put_tiled(kd_ref, j, k_d)
            put_tiled(b_ref, j, kk * a)
            s_bonus = s_bonus + r * k_d * rk_ref[:, sl]
    inv = lax.rsqrt(jnp.maximum(_head_fold(s_kk, n_heads), 1e-24))
    s_bonus = _head_fold(s_bonus, n_heads)
    for c in range(n_keys):
        sl = _slab(c)
        nkk_ref[:, sl] = -nkk_ref[:, sl] * inv
        b0_ref[:, sl] = b0_ref[:, sl] * inv
        b1_ref[:, sl] = b1_ref[:, sl] * inv
    for j in range(nslab):
        bonus_ref[:, _slab(j)] = s_bonus * v_ref[:, _slab(j)]


def rwkv_elem(tt, dec2, iclr2, dec0, iclr0, r, k, v, kk, ka, rk, *, rd, ri, n_heads):
    t, d = r.shape
    bm = _pick(t, 32, 16)
    tw = tt.shape[1]
    dt = (d // n_heads) * LANES
    tile = pl.BlockSpec((bm, d), lambda i: (i, 0))
    tiled = pl.BlockSpec((bm, dt), lambda i: (i, 0))
    row = pl.BlockSpec((1, d), lambda i: (0, 0))
    return pl.pallas_call(
        functools.partial(_rwkv_elem_kernel, rd=rd, ri=ri, n_heads=n_heads),
        grid=(t // bm,),
        in_specs=[
            pl.BlockSpec((bm, tw), lambda i: (i, 0)),
            pl.BlockSpec((2, rd, d), lambda i: (0, 0, 0)),
            pl.BlockSpec((2, ri, d), lambda i: (0, 0, 0)),
            pl.BlockSpec((2, d), lambda i: (0, 0)),
            pl.BlockSpec((2, d), lambda i: (0, 0)),
            tile, tile, tile, row, row, row,
        ],
        out_specs=[tiled] * 8 + [tile],
        out_shape=[jax.ShapeDtypeStruct((t, dt), F32)] * 8 + [jax.ShapeDtypeStruct((t, d), F32)],
        compiler_params=_params("arbitrary"),
        name="rwkv_elem",
    )(tt, dec2, iclr2, dec0, iclr0, r, k, v, kk, ka, rk)


def _scan_kernel(rf, wf, kf, vf, af, bf, rb, wb, kb, vb, ab, bb, yf_ref, yb_ref,
                 sf_ref, sb_ref, *, tb, n):
    @pl.when(pl.program_id(0) == 0)
    def _():
        sf_ref[...] = jnp.zeros_like(sf_ref)
        sb_ref[...] = jnp.zeros_like(sb_ref)

    nvg = sf_ref.shape[1]
    rows = min(SUBLANES, nvg)
    nvb = nvg // rows

    def vslice(i):
        return pl.ds(i * rows, rows)

    def bcast(ref, t, kc):
        return jnp.broadcast_to(ref[t, pl.ds(kc, 1), :], (rows, LANES))

    def zeros():
        return [jnp.zeros((rows, LANES), F32) for _ in range(nvb)]

    def state_dot_a(t, a_ref, s_ref):
        sa = zeros()
        for kc in range(n):
            a_k = bcast(a_ref, t, kc)
            for i in range(nvb):
                sa[i] = sa[i] + s_ref[kc, vslice(i), :] * a_k
        return tuple(sa)

    def one_direction(t, t_next, r_ref, w_ref, k_ref, v_ref, a_ref, b_ref, y_ref, s_ref, sa):
        vv = [v_ref[t, vslice(i), :] for i in range(nvb)]
        y, sa_next = zeros(), zeros()
        for kc in range(n):
            w_k, b_k, k_k = bcast(w_ref, t, kc), bcast(b_ref, t, kc), bcast(k_ref, t, kc)
            r_k, a_k = bcast(r_ref, t, kc), bcast(a_ref, t_next, kc)
            for i in range(nvb):
                s_new = s_ref[kc, vslice(i), :] * w_k + sa[i] * b_k + vv[i] * k_k
                s_ref[kc, vslice(i), :] = s_new
                y[i] = y[i] + s_new * r_k
                sa_next[i] = sa_next[i] + s_new * a_k
        for i in range(nvb):
            y_ref[t, vslice(i), :] = y[i]
        return tuple(sa_next)

    def step(t, carry):
        sa_f, sa_b = carry
        sa_f = one_direction(t, jnp.minimum(t + 1, tb - 1), rf, wf, kf, vf, af, bf, yf_ref,
                             sf_ref, sa_f)
        u = tb - 1 - t
        sa_b = one_direction(u, jnp.maximum(u - 1, 0), rb, wb, kb, vb, ab, bb, yb_ref,
                             sb_ref, sa_b)
        return sa_f, sa_b

    init = (state_dot_a(0, af, sf_ref), state_dot_a(tb - 1, ab, sb_ref))
    lax.fori_loop(0, tb, step, init)


def rwkv_scan(r, w0, w1, k0, k1, v, a, b0, b1, *, n_lat):
    t, d = v.shape
    n = r.shape[1] // LANES
    nvg = d // LANES
    tb = _pick(math.gcd(n_lat, t - n_lat), 32, 1)
    nb = t // tb
    first = n_lat // tb
    fwd = lambda i: ((i + first) % nb, 0, 0)
    bwd = lambda i: (nb - 1 - i, 0, 0)
    kview = lambda x: x.reshape(t, n, LANES)
    vview = lambda x: x.reshape(t, nvg, LANES)

    def specs(imap):
        ks = pl.BlockSpec((tb, n, LANES), imap)
        vs = pl.BlockSpec((tb, nvg, LANES), imap)
        return [ks, ks, ks, vs, ks, ks], vs

    in_f, out_f = specs(fwd)
    in_b, out_b = specs(bwd)
    y_f, y_b = pl.pallas_call(
        functools.partial(_scan_kernel, tb=tb, n=n),
        grid=(nb,),
        in_specs=in_f + in_b,
        out_specs=[out_f, out_b],
        out_shape=[jax.ShapeDtypeStruct((t, nvg, LANES), F32)] * 2,
        scratch_shapes=[pltpu.VMEM((n, nvg, LANES), F32)] * 2,
        compiler_params=_params("arbitrary"),
        name="rwkv_scan",
    )(kview(r), kview(w0), kview(k0), vview(v), kview(a), kview(b0),
      kview(r), kview(w1), kview(k1), vview(v), kview(a), kview(b1))
    return y_f.reshape(t, d), y_b.reshape(t, d)


def _rwkv_post_kernel(yf_ref, yb_ref, bonus_ref, g_ref, lw_ref, lb_ref, o_ref, *, n_heads):
    bm, d = yf_ref.shape
    nslab = d // LANES
    inv_n = 1.0 / (d // n_heads)
    s1 = jnp.zeros((bm, LANES), F32)
    for j in range(nslab):
        s1 = s1 + (yf_ref[:, _slab(j)] + yb_ref[:, _slab(j)])
    mu = _head_fold(s1, n_heads) * inv_n
    s2 = jnp.zeros((bm, LANES), F32)
    for j in range(nslab):
        yc = yf_ref[:, _slab(j)] + yb_ref[:, _slab(j)] - mu
        s2 = s2 + yc * yc
    rstd = lax.rsqrt(_head_fold(s2, n_heads) * inv_n + LNX_EPS)
    for j in range(nslab):
        sl = _slab(j)
        yn = (yf_ref[:, sl] + yb_ref[:, sl] - mu) * rstd
        out = yn * lw_ref[:, sl] + lb_ref[:, sl] + bonus_ref[:, sl]
        o_ref[:, sl] = (out * g_ref[:, sl].astype(F32)).astype(o_ref.dtype)


def rwkv_post(yf, yb, bonus, g, lnx_w, lnx_b, *, n_heads):
    t, d = yf.shape
    bm = _pick(t, 128, 16)
    tile = pl.BlockSpec((bm, d), lambda i: (i, 0))
    row = pl.BlockSpec((1, d), lambda i: (0, 0))
    return pl.pallas_call(
        functools.partial(_rwkv_post_kernel, n_heads=n_heads),
        grid=(t // bm,),
        in_specs=[tile, tile, tile, tile, row, row],
        out_specs=tile,
        out_shape=jax.ShapeDtypeStruct((t, d), BF16),
        compiler_params=_params("arbitrary"),
        name="rwkv_post",
    )(yf, yb, bonus, g, lnx_w.reshape(1, d), lnx_b.reshape(1, d))


def _qk_epi(scale):
    def epi(acc, ex, i_m):
        gain_ref, cos_ref, sin_ref = ex
        cos, sin = cos_ref[...], sin_ref[...]
        gain = gain_ref[...] * scale
        lane = lax.broadcasted_iota(jnp.int32, cos.shape, 1)
        even = (lane % 2) == 0
        outs = []
        for j in range(acc.shape[1] // HEAD_DIM):
            q = acc[:, j * HEAD_DIM:(j + 1) * HEAD_DIM]
            ms = jnp.mean(q * q, axis=-1, keepdims=True)
            q = q * lax.rsqrt(ms + NORM_EPS) * gain
            partner = jnp.where(even, pltpu.roll(q, HEAD_DIM - 1, 1), pltpu.roll(q, 1, 1))
            outs.append(q * cos + partner * sin)
        return jnp.concatenate(outs, axis=1)
    return epi


def qk_matmul(x, w, gain, cos_t, sin_t, *, rows, scale, lead=(), name="qk_mm"):
    n = w.shape[-1]
    bm = _pick(rows, 1024, 16)
    bn = _pick(n, 512, LANES)
    extras = [
        (gain.reshape(1, HEAD_DIM), (1, HEAD_DIM), lambda j, i, k: (0, 0)),
        (cos_t, (bm, HEAD_DIM), lambda j, i, k: (i, 0)),
        (sin_t, (bm, HEAD_DIM), lambda j, i, k: (i, 0)),
    ]
    return matmul(x, w, rows=rows, lead=lead, out_dtype=BF16, epi=_qk_epi(scale),
                  extras=extras, bm=bm, bn=bn, name=name)


def _attn_kernel(q_ref, k_ref, v_ref, o_ref, *, groups, bk, nchunks):
    bq = q_ref.shape[0]
    q = jnp.concatenate([q_ref[:, g * HEAD_DIM:(g + 1) * HEAD_DIM] for g in range(groups)], axis=0)
    rows = groups * bq

    def body(c, carry):
        m, l, acc = carry
        start = pl.multiple_of(c * bk, bk)
        kc = k_ref[pl.ds(start, bk), :]
        vc = v_ref[pl.ds(start, bk), :]
        s = lax.dot_general(q, kc, (((1,), (1,)), ((), ())), preferred_element_type=F32)
        m_new = jnp.maximum(m, jnp.max(s, axis=1, keepdims=True))
        alpha = jnp.exp(m - m_new)
        p = jnp.exp(s - m_new)
        l = alpha * l + jnp.sum(p, axis=1, keepdims=True)
        acc = alpha * acc + _bdot(p.astype(BF16), vc)
        return m_new, l, acc

    init = (jnp.full((rows, 1), -jnp.inf, F32), jnp.zeros((rows, 1), F32),
            jnp.zeros((rows, HEAD_DIM), F32))
    _, l, acc = lax.fori_loop(0, nchunks, body, init)
    out = acc / l
    for g in range(groups):
        o_ref[:, g * HEAD_DIM:(g + 1) * HEAD_DIM] = out[g * bq:(g + 1) * bq].astype(o_ref.dtype)


def attention(q, k, v, *, groups):
    s, dq = q.shape
    tk, dkv = k.shape
    n_kv = dkv // HEAD_DIM
    bq = _pick(s, 256, 16)
    bk = _pick(tk, 2816, 256) if tk % 256 == 0 else tk
    return pl.pallas_call(
        functools.partial(_attn_kernel, groups=groups, bk=bk, nchunks=tk // bk),
        grid=(n_kv, s // bq),
        in_specs=[
            pl.BlockSpec((bq, groups * HEAD_DIM), lambda h, i: (i, h)),
            pl.BlockSpec((tk, HEAD_DIM), lambda h, i: (0, h)),
            pl.BlockSpec((tk, HEAD_DIM), lambda h, i: (0, h)),
        ],
        out_specs=pl.BlockSpec((bq, groups * HEAD_DIM), lambda h, i: (i, h)),
        out_shape=jax.ShapeDtypeStruct((s, dq), BF16),
        compiler_params=_params("arbitrary", "arbitrary"),
        name="attention",
    )(q, k, v)


ROUTE_BLOCK = 256


def _route_plan(idx, wgt, n_experts):
    s = idx.shape[0]
    bm = ROUTE_BLOCK
    p_rows = 2 * s + n_experts * bm
    e_flat = idx.reshape(-1)
    onehot = (e_flat[:, None] == jnp.arange(n_experts, dtype=jnp.int32)[None, :]).astype(jnp.int32)
    counts = jnp.sum(onehot, axis=0)
    padded = ((counts + bm - 1) // bm) * bm
    ends = jnp.cumsum(padded)
    offs = ends - padded
    rank = jnp.cumsum(onehot, axis=0) - onehot
    pos = jnp.sum((offs[None, :] + rank) * onehot, axis=1)
    row_token = jnp.zeros((p_rows,), jnp.int32).at[pos].set(jnp.arange(2 * s, dtype=jnp.int32) // 2)
    row_weight = jnp.zeros((p_rows,), F32).at[pos].set(wgt.reshape(-1))
    blk_start = jnp.arange(p_rows // bm, dtype=jnp.int32) * bm
    block_expert = jnp.minimum(jnp.searchsorted(ends, blk_start, side="right"),
                               n_experts - 1).astype(jnp.int32)
    return pos.reshape(s, 2).astype(jnp.int32), row_token, row_weight.reshape(p_rows, 1), block_expert


def _row_copy(src_hbm, row, dst_vmem, slot, sem):
    return pltpu.make_async_copy(src_hbm.at[row], dst_vmem.at[slot], sem)


def _gather_rows_kernel(tok_ref, x_hbm, o_ref, sem, *, rows_per_step):
    base = pl.program_id(0) * rows_per_step

    def start(i, c):
        _row_copy(x_hbm, tok_ref[base + i], o_ref, i, sem).start()
        return c

    def wait(i, c):
        _row_copy(x_hbm, 0, o_ref, i, sem).wait()
        return c

    lax.fori_loop(0, rows_per_step, start, 0)
    lax.fori_loop(0, rows_per_step, wait, 0)


def gather_rows(x, row_token):
    s, d = x.shape
    p_rows = row_token.shape[0]
    rows_per_step = ROUTE_BLOCK
    x3 = x.reshape(s, d // LANES, LANES)
    grid_spec = pltpu.PrefetchScalarGridSpec(
        num_scalar_prefetch=1,
        grid=(p_rows // rows_per_step,),
        in_specs=[pl.BlockSpec(memory_space=pl.ANY)],
        out_specs=pl.BlockSpec((rows_per_step, d // LANES, LANES), lambda i, tok: (i, 0, 0)),
        scratch_shapes=[pltpu.SemaphoreType.DMA(())],
    )
    out = pl.pallas_call(
        functools.partial(_gather_rows_kernel, rows_per_step=rows_per_step),
        grid_spec=grid_spec,
        out_shape=jax.ShapeDtypeStruct((p_rows, d // LANES, LANES), x.dtype),
        compiler_params=_params("arbitrary"),
        name="moe_gather",
    )(row_token, x3)
    return out.reshape(p_rows, d)


def _combine_kernel(pos_ref, y_hbm, h_ref, g_ref, o_ref, buf, sem, *, rows_per_step):
    base = pl.program_id(0) * rows_per_step

    def start(i, c):
        for slot in range(TOP_K):
            _row_copy(y_hbm, pos_ref[(base + i) * TOP_K + slot], buf.at[slot], i, sem).start()
        return c

    def wait(i, c):
        for slot in range(TOP_K):
            _row_copy(y_hbm, 0, buf.at[slot], i, sem).wait()
        return c

    lax.fori_loop(0, rows_per_step, start, 0)
    lax.fori_loop(0, rows_per_step, wait, 0)
    o_ref[...] = h_ref[...] + g_ref[...] * (buf[0] + buf[1])


def combine_rows(y, pos, h, gate):
    s, d = h.shape
    p_rows = y.shape[0]
    rows_per_step = _pick(s, 128)
    slab = (d // LANES, LANES)
    blk = pl.BlockSpec((rows_per_step,) + slab, lambda i, pos: (i, 0, 0))
    grid_spec = pltpu.PrefetchScalarGridSpec(
        num_scalar_prefetch=1,
        grid=(s // rows_per_step,),
        in_specs=[pl.BlockSpec(memory_space=pl.ANY), blk,
                  pl.BlockSpec(slab, lambda i, pos: (0, 0))],
        out_specs=blk,
        scratch_shapes=[pltpu.VMEM((TOP_K, rows_per_step) + slab, F32),
                        pltpu.SemaphoreType.DMA(())],
    )
    out = pl.pallas_call(
        functools.partial(_combine_kernel, rows_per_step=rows_per_step),
        grid_spec=grid_spec,
        out_shape=jax.ShapeDtypeStruct((s,) + slab, F32),
        compiler_params=_params("arbitrary"),
        name="moe_combine",
    )(pos.reshape(-1), y.reshape((p_rows,) + slab), h.reshape((s,) + slab), gate.reshape(slab))
    return out.reshape(s, d)


def _head_minor_cols(w, n_heads):
    lead = w.shape[:-1]
    n = w.shape[-1] // n_heads
    return jnp.swapaxes(w.reshape(lead + (n_heads, n)), -1, -2).reshape(lead + (n_heads * n,))


def _head_minor_rows(w, n_heads):
    rows, cols = w.shape
    return jnp.swapaxes(w.reshape(n_heads, rows // n_heads, cols), 0, 1).reshape(rows, cols)


def _pad_cols(w, mult):
    pad = (-w.shape[-1]) % mult
    return jnp.pad(w, [(0, 0)] * (w.ndim - 1) + [(0, pad)]) if pad else w


def kernel(x, c, ctx, c_ctx, ada_w, ada_b, norm_gains, rwkv_mix, rwkv_wr, rwkv_wk, rwkv_wv, rwkv_wo, rwkv_dec0, rwkv_dec1, rwkv_dec2, rwkv_iclr0, rwkv_iclr1, rwkv_iclr2, rwkv_gate1, rwkv_gate2, rwkv_kk, rwkv_ka, rwkv_rk, rwkv_lnx_w, rwkv_lnx_b, ffn_w_gate, ffn_w_up, ffn_w_down, attn_wq, attn_wk, attn_wv, attn_wo, attn_q_gain, attn_k_gain, moe_router, moe_w_gate, moe_w_up, moe_w_down, final_norm):
    batch, n_lat, d = x.shape
    n_ctx = ctx.shape[1]
    depth = ada_w.shape[0]
    assert batch == 1 and depth == 2 and rwkv_dec0.shape[1] == 2
    t = n_lat + n_ctx
    n_experts = moe_router.shape[-1]
    assert n_experts <= LANES and TOP_K == 2

    h = jnp.concatenate([x[0], ctx[0]], axis=0)

    cond8 = jnp.zeros((SUBLANES, d), F32).at[0].set(c[0]).at[1].set(c_ctx)
    mod = ada_params(cond8, ada_w, ada_b)[:, :2, :].reshape(depth, 2, N_MOD, d)

    xr, xw, xk, xv, xa, xg = rwkv_prep(h, norm_gains[0, 0], mod[0], rwkv_mix[0], n_lat=n_lat)
    nh = d // RWKV_HEAD
    hm = functools.partial(_head_minor_cols, n_heads=nh)
    r = matmul(xr, hm(rwkv_wr[0]).astype(BF16), rows=t, name="rwkv_r")
    k = matmul(xk, hm(rwkv_wk[0]).astype(BF16), rows=t, name="rwkv_k")
    v = matmul(xv, hm(rwkv_wv[0]).astype(BF16), rows=t, name="rwkv_v")

    rd, ri = rwkv_dec1.shape[-1], rwkv_iclr1.shape[-1]
    dec1 = jnp.concatenate([rwkv_dec1[0, 0], rwkv_dec1[0, 1]], axis=1)
    iclr1 = jnp.concatenate([rwkv_iclr1[0, 0], rwkv_iclr1[0, 1]], axis=1)
    gate1 = _pad_cols(rwkv_gate1[0], LANES)
    rg = gate1.shape[1]
    gate2 = jnp.pad(hm(rwkv_gate2[0]), ((0, rg - rwkv_gate2.shape[1]), (0, 0)))
    tw = matmul(xw, dec1, rows=t, out_dtype=BF16, epi=lambda a, e, i: jnp.tanh(a), name="lora_w")
    ta = matmul(xa, iclr1, rows=t, out_dtype=BF16, name="lora_a")
    tg = matmul(xg, gate1, rows=t, out_dtype=BF16, epi=lambda a, e, i: _sigmoid(a), name="lora_g")
    g = matmul(tg, gate2, rows=t, out_dtype=BF16, name="rwkv_gate")

    tt = jnp.concatenate([tw, ta], axis=1)
    w0, w1, k0, k1, b0, b1, nkk, rt, bonus = rwkv_elem(
        tt, hm(rwkv_dec2[0]).astype(BF16), hm(rwkv_iclr2[0]).astype(BF16), hm(rwkv_dec0[0]),
        hm(rwkv_iclr0[0]), r, k, v, hm(rwkv_kk[0]).reshape(1, d), hm(rwkv_ka[0]).reshape(1, d),
        hm(rwkv_rk[0].reshape(d)).reshape(1, d), rd=rd, ri=ri, n_heads=nh)

    y_f, y_b = rwkv_scan(rt, w0, w1, k0, k1, v, nkk, b0, b1, n_lat=n_lat)
    yo = rwkv_post(y_f, y_b, bonus, g, hm(rwkv_lnx_w[0]), hm(rwkv_lnx_b[0]), n_heads=nh)
    h = residual_matmul(yo, _head_minor_rows(rwkv_wo[0], nh).astype(BF16), h, mod[0], rows=t,
                        n_lat=n_lat, gate_idx=2, name="rwkv_out")

    nf = norm_mod(h, norm_gains[0, 1], mod[0], rows=t, n_lat=n_lat, sub=1)
    hid = glu(nf, ffn_w_gate, ffn_w_up, rows=t)
    h = residual_matmul(hid, ffn_w_down.astype(BF16), h, mod[0], rows=t, n_lat=n_lat, gate_idx=5,
                        lead=(0,), bk=_pick(ffn_w_down.shape[1], 5504, LANES), name="ffn_down")

    na = norm_mod(h, norm_gains[1, 0], mod[1], rows=t, n_lat=n_lat, sub=0)
    pairs = HEAD_DIM // 4
    inv_freq = ROPE_THETA ** (-jnp.arange(pairs, dtype=F32) / pairs)
    pos = jnp.arange(n_lat)
    ang = jnp.concatenate([(pos // GRID_W).astype(F32)[:, None] * inv_freq,
                           (pos % GRID_W).astype(F32)[:, None] * inv_freq], axis=-1)
    sign = jnp.tile(jnp.asarray([-1.0, 1.0], F32), HEAD_DIM // 2)
    cos_t = jnp.concatenate([jnp.repeat(jnp.cos(ang), 2, axis=-1),
                             jnp.ones((n_ctx, HEAD_DIM), F32)], axis=0)
    sin_t = jnp.concatenate([jnp.repeat(jnp.sin(ang), 2, axis=-1) * sign,
                             jnp.zeros((n_ctx, HEAD_DIM), F32)], axis=0)
    q = qk_matmul(na, attn_wq, attn_q_gain[0], cos_t, sin_t, rows=n_lat,
                  scale=HEAD_DIM ** -0.5, lead=(0,), name="attn_q")
    kx = qk_matmul(na, attn_wk, attn_k_gain[0], cos_t, sin_t, rows=t, scale=1.0, lead=(0,),
                   name="attn_k")
    vx = matmul(na, attn_wv, rows=t, lead=(0,), out_dtype=BF16, name="attn_v")
    groups = attn_wq.shape[-1] // attn_wk.shape[-1]
    o = attention(q, kx, vx, groups=groups)
    h = residual_matmul(o, attn_wo, h, mod[1], rows=n_lat, n_lat=n_lat, gate_idx=2, lead=(0,),
                        name="attn_out")

    nm, ridx, rwgt = norm_router(h, norm_gains[1, 1], mod[1, 0], _pad_cols(moe_router[0], LANES),
                                 rows=n_lat, n_experts=n_experts)
    pos, row_token, row_weight, block_expert = _route_plan(ridx[:, :TOP_K], rwgt[:, :TOP_K],
                                                           n_experts)
    p_rows = row_token.shape[0]
    xs = gather_rows(nm, row_token)
    hid = glu(xs, moe_w_gate[0], moe_w_up[0], rows=p_rows, bm=ROUTE_BLOCK,
              block_expert=block_expert, row_scale=row_weight)
    ys = matmul(hid, moe_w_down[0], rows=p_rows, block_group=block_expert, bm=ROUTE_BLOCK,
                name="moe_down")
    h = combine_rows(ys, pos, h[:n_lat], mod[1, 0, 5])

    return final_rms_norm(h, final_norm, rows=n_lat)[None]
```

```python
import functools
import math

import numpy as np
import jax
import jax.numpy as jnp
from jax import lax
from jax.experimental import pallas as pl
from jax.experimental.pallas import tpu as pltpu

F32 = jnp.float32
BF16 = jnp.bfloat16

N_MOD = 6
NORM_EPS = 1e-6
LNX_EPS = 64e-5
RWKV_HEAD = 64
HEAD_DIM = 128
GRID_W = 64
ROPE_THETA = 10000.0
TOP_K = 2
LANES = 128
SUBLANES = 8
VMEM_LIMIT = 56 * 1024 * 1024


def _pick(n, target, mult=SUBLANES):
    best = None
    for b in range(mult, min(n, target) + 1, mult):
        if n % b == 0:
            best = b
    return best if best is not None else n


def _params(*sem):
    return pltpu.CompilerParams(dimension_semantics=sem, vmem_limit_bytes=VMEM_LIMIT)


def _sigmoid(x):
    return 1.0 / (1.0 + jnp.exp(-x))


def _silu(x):
    return x * _sigmoid(x)


def _bdot(a, b):
    return jnp.dot(a, b, preferred_element_type=F32)


def _ada_kernel(c_ref, w_ref, b_ref, o_ref):
    s = _silu(c_ref[...]).astype(BF16)
    o_ref[...] = _bdot(s, w_ref[...].astype(BF16)) + b_ref[...]


def ada_params(cond8, ada_w, ada_b):
    depth, d, n = ada_w.shape
    bn = _pick(n, 512, LANES)
    return pl.pallas_call(
        _ada_kernel,
        grid=(depth, n // bn),
        in_specs=[
            pl.BlockSpec((SUBLANES, d), lambda l, j: (0, 0)),
            pl.BlockSpec((None, d, bn), lambda l, j: (l, 0, j)),
            pl.BlockSpec((None, 1, bn), lambda l, j: (l, 0, j)),
        ],
        out_specs=pl.BlockSpec((None, SUBLANES, bn), lambda l, j: (l, 0, j)),
        out_shape=jax.ShapeDtypeStruct((depth, SUBLANES, n), F32),
        compiler_params=_params("arbitrary", "arbitrary"),
        name="ada_params",
    )(cond8, ada_w, ada_b.reshape(depth, 1, n))


def _norm_mod(x, gain, shift, scale):
    ms = jnp.mean(x * x, axis=-1, keepdims=True)
    return (x * lax.rsqrt(ms + NORM_EPS) * gain) * (1.0 + scale) + shift


def _norm_mod_kernel(h_ref, g_ref, m_ref, o_ref, *, sub):
    n = _norm_mod(h_ref[...], g_ref[...], m_ref[3 * sub:3 * sub + 1, :],
                  m_ref[3 * sub + 1:3 * sub + 2, :])
    o_ref[...] = n.astype(o_ref.dtype)


def norm_mod(h, gain, mod, *, rows, n_lat, sub):
    t, d = h.shape
    bm = _pick(math.gcd(n_lat, t - n_lat) if t > n_lat else n_lat, 256)
    nlat_blocks = n_lat // bm
    return pl.pallas_call(
        functools.partial(_norm_mod_kernel, sub=sub),
        grid=(rows // bm,),
        in_specs=[
            pl.BlockSpec((bm, d), lambda i: (i, 0)),
            pl.BlockSpec((1, d), lambda i: (0, 0)),
            pl.BlockSpec((None, N_MOD, d), lambda i: (jnp.where(i >= nlat_blocks, 1, 0), 0, 0)),
        ],
        out_specs=pl.BlockSpec((bm, d), lambda i: (i, 0)),
        out_shape=jax.ShapeDtypeStruct((rows, d), BF16),
        compiler_params=_params("arbitrary"),
        name="norm_mod",
    )(h, gain.reshape(1, d), mod)


def _norm_router_kernel(h_ref, g_ref, m_ref, r_ref, o_ref, idx_ref, wgt_ref, *, n_experts):
    n = _norm_mod(h_ref[...], g_ref[...], m_ref[3:4, :], m_ref[4:5, :])
    o_ref[...] = n.astype(o_ref.dtype)
    logits = jnp.dot(n, r_ref[...], precision=lax.Precision.HIGHEST,
                     preferred_element_type=F32)
    lane = lax.broadcasted_iota(jnp.int32, logits.shape, 1)
    neg = jnp.float32(-jnp.inf)
    lg = jnp.where(lane < n_experts, logits, neg)
    m1 = jnp.max(lg, axis=-1, keepdims=True)
    i1 = jnp.min(jnp.where(lg == m1, lane, LANES), axis=-1, keepdims=True)
    lg2 = jnp.where(lane == i1, neg, lg)
    m2 = jnp.max(lg2, axis=-1, keepdims=True)
    i2 = jnp.min(jnp.where(lg2 == m2, lane, LANES), axis=-1, keepdims=True)
    e = jnp.exp(m2 - m1)
    w1 = 1.0 / (1.0 + e)
    idx_ref[...] = jnp.where(lane == 0, i1, jnp.where(lane == 1, i2, 0))
    wgt_ref[...] = jnp.where(lane == 0, w1, jnp.where(lane == 1, 1.0 - w1, 0.0))


def norm_router(h, gain, mod_lat, router_pad, *, rows, n_experts):
    t, d = h.shape
    bm = _pick(rows, 256)
    lane_blk = pl.BlockSpec((bm, LANES), lambda i: (i, 0))
    return pl.pallas_call(
        functools.partial(_norm_router_kernel, n_experts=n_experts),
        grid=(rows // bm,),
        in_specs=[
            pl.BlockSpec((bm, d), lambda i: (i, 0)),
            pl.BlockSpec((1, d), lambda i: (0, 0)),
            pl.BlockSpec((N_MOD, d), lambda i: (0, 0)),
            pl.BlockSpec((d, LANES), lambda i: (0, 0)),
        ],
        out_specs=[pl.BlockSpec((bm, d), lambda i: (i, 0)), lane_blk, lane_blk],
        out_shape=[jax.ShapeDtypeStruct((rows, d), BF16),
                   jax.ShapeDtypeStruct((rows, LANES), jnp.int32),
                   jax.ShapeDtypeStruct((rows, LANES), F32)],
        compiler_params=_params("arbitrary"),
        name="norm_router",
    )(h, gain.reshape(1, d), mod_lat, router_pad)


def _final_norm_kernel(h_ref, g_ref, o_ref):
    x = h_ref[...]
    ms = jnp.mean(x * x, axis=-1, keepdims=True)
    o_ref[...] = x * lax.rsqrt(ms + NORM_EPS) * g_ref[...]


def final_rms_norm(h, gain, *, rows):
    t, d = h.shape
    bm = _pick(rows, 256)
    return pl.pallas_call(
        _final_norm_kernel,
        grid=(rows // bm,),
        in_specs=[pl.BlockSpec((bm, d), lambda i: (i, 0)),
                  pl.BlockSpec((1, d), lambda i: (0, 0))],
        out_specs=pl.BlockSpec((bm, d), lambda i: (i, 0)),
        out_shape=jax.ShapeDtypeStruct((rows, d), F32),
        compiler_params=_params("arbitrary"),
        name="final_norm",
    )(h, gain.reshape(1, d))


def _rwkv_prep_kernel(h_ref, hp_ref, hn_ref, g_ref, m_ref, mix_ref, *o_refs,
                      nlat_blocks, nblocks):
    i = pl.program_id(0)
    gain, shift, scale = g_ref[...], m_ref[0:1, :], m_ref[1:2, :]
    n = _norm_mod(h_ref[...], gain, shift, scale)
    n_prev = _norm_mod(hp_ref[...], gain, shift, scale)[SUBLANES - 1:SUBLANES, :]
    n_next = _norm_mod(hn_ref[...], gain, shift, scale)[0:1, :]
    first = jnp.logical_or(i == 0, i == nlat_blocks)
    last = jnp.logical_or(i == nlat_blocks - 1, i == nblocks - 1)
    n_prev = jnp.where(first, 0.0, n_prev)
    n_next = jnp.where(last, 0.0, n_next)
    bm = n.shape[0]
    row = lax.broadcasted_iota(jnp.int32, n.shape, 0)
    prev = jnp.where(row == 0, n_prev, pltpu.roll(n, 1, 0))
    nxt = jnp.where(row == bm - 1, n_next, pltpu.roll(n, bm - 1, 0))
    dh = 0.5 * (prev + nxt) - n
    for j, o_ref in enumerate(o_refs):
        o_ref[...] = (n + dh * mix_ref[j:j + 1, :]).astype(o_ref.dtype)


def rwkv_prep(h, gain, mod, mix, *, n_lat):
    t, d = h.shape
    bm = _pick(math.gcd(n_lat, t - n_lat), 128)
    nblocks, nlat_blocks = t // bm, n_lat // bm
    r8 = bm // SUBLANES
    n8 = t // SUBLANES
    return pl.pallas_call(
        functools.partial(_rwkv_prep_kernel, nlat_blocks=nlat_blocks, nblocks=nblocks),
        grid=(nblocks,),
        in_specs=[
            pl.BlockSpec((bm, d), lambda i: (i, 0)),
            pl.BlockSpec((SUBLANES, d), lambda i: (jnp.maximum(i * r8 - 1, 0), 0)),
            pl.BlockSpec((SUBLANES, d), lambda i: (jnp.minimum((i + 1) * r8, n8 - 1), 0)),
            pl.BlockSpec((1, d), lambda i: (0, 0)),
            pl.BlockSpec((None, N_MOD, d), lambda i: (jnp.where(i >= nlat_blocks, 1, 0), 0, 0)),
            pl.BlockSpec((6, d), lambda i: (0, 0)),
        ],
        out_specs=[pl.BlockSpec((bm, d), lambda i: (i, 0))] * 6,
        out_shape=[jax.ShapeDtypeStruct((t, d), BF16)] * 6,
        compiler_params=_params("arbitrary"),
        name="rwkv_prep",
    )(h, h, h, gain.reshape(1, d), mod, mix)


def _mm_kernel(*refs, nk, n_extra, epi, grouped):
    if grouped:
        refs = refs[1:]
    x_ref, w_ref = refs[0], refs[1]
    extras = refs[2:2 + n_extra]
    o_ref = refs[2 + n_extra]
    i_m = pl.program_id(1)
    part = _bdot(x_ref[...].astype(BF16), w_ref[...].astype(BF16))
    if nk == 1:
        o_ref[...] = epi(part, extras, i_m).astype(o_ref.dtype)
    else:
        acc_ref = refs[3 + n_extra]
        k = pl.program_id(2)

        @pl.when(k == 0)
        def _():
            acc_ref[...] = part

        @pl.when(k > 0)
        def _():
            acc_ref[...] += part

        @pl.when(k == nk - 1)
        def _():
            o_ref[...] = epi(acc_ref[...], extras, i_m).astype(o_ref.dtype)


def matmul(x, w, *, rows, lead=(), block_group=None, out_dtype=F32, epi=None, extras=(),
           bm=1056, bn=512, bk=4096, name="matmul"):
    kdim = x.shape[1]
    n = w.shape[-1]
    bm = _pick(rows, bm, 16)
    bn = _pick(n, bn, LANES)
    bk = _pick(kdim, bk, LANES)
    nk = kdim // bk
    if epi is None:
        epi = lambda acc, ex, i_m: acc
    grouped = block_group is not None
    if grouped:
        w_spec = pl.BlockSpec((None, bk, bn), lambda j, i, k, grp: (grp[i], k, j))
    else:
        w_spec = pl.BlockSpec((None,) * len(lead) + (bk, bn),
                              lambda j, i, k: tuple(lead) + (k, j))
    in_specs = [pl.BlockSpec((bm, bk), lambda j, i, k, *_: (i, k)), w_spec]
    args = [x, w]
    for arr, bshape, imap in extras:
        in_specs.append(pl.BlockSpec(bshape, imap))
        args.append(arr)
    if grouped:
        args = [block_group] + args
    grid_spec = pltpu.PrefetchScalarGridSpec(
        num_scalar_prefetch=1 if grouped else 0,
        grid=(n // bn, rows // bm, nk),
        in_specs=in_specs,
        out_specs=pl.BlockSpec((bm, bn), lambda j, i, k, *_: (i, j)),
        scratch_shapes=[pltpu.VMEM((bm, bn), F32)] if nk > 1 else [],
    )
    return pl.pallas_call(
        functools.partial(_mm_kernel, nk=nk, n_extra=len(extras), epi=epi, grouped=grouped),
        grid_spec=grid_spec,
        out_shape=jax.ShapeDtypeStruct((rows, n), out_dtype),
        compiler_params=_params("arbitrary", "arbitrary", "arbitrary"),
        name=name,
    )(*args)


def _residual_epi(n_lat, gate_idx):
    def epi(acc, ex, i_m):
        h_ref, m_ref = ex
        bm = acc.shape[0]
        row = i_m * bm + lax.broadcasted_iota(jnp.int32, (bm, 1), 0)
        gate = jnp.where(row < n_lat, m_ref[0, gate_idx:gate_idx + 1, :],
                         m_ref[1, gate_idx:gate_idx + 1, :])
        return h_ref[...] + gate * acc
    return epi


def residual_matmul(x, w, h, mod, *, rows, n_lat, gate_idx, lead=(), bk=4096, name="residual_mm"):
    n = w.shape[-1]
    bm = _pick(rows, 1056, 16)
    bn = _pick(n, 512, LANES)
    extras = [
        (h, (bm, bn), lambda j, i, k: (i, j)),
        (mod, (2, N_MOD, bn), lambda j, i, k: (0, 0, j)),
    ]
    return matmul(x, w, rows=rows, lead=lead, epi=_residual_epi(n_lat, gate_idx),
                  extras=extras, bm=bm, bn=bn, bk=bk, name=name)


def _glu_kernel(*refs, grouped):
    if grouped:
        _, x_ref, wg_ref, wu_ref, scale_ref, o_ref = refs
    else:
        x_ref, wg_ref, wu_ref, o_ref = refs
    x = x_ref[...]
    g = _bdot(x, wg_ref[...].astype(BF16))
    u = _bdot(x, wu_ref[...].astype(BF16))
    hdn = _silu(g) * u
    if grouped:
        hdn = hdn * scale_ref[...]
    o_ref[...] = hdn.astype(o_ref.dtype)


def glu(x, wg, wu, *, rows, bm=1056, bn=256, block_expert=None, row_scale=None):
    n_exp, kdim, f = wg.shape
    grouped = block_expert is not None
    assert grouped or n_exp == 1
    bm = _pick(rows, bm, 16)
    bn = _pick(f, bn, LANES)
    w_map = (lambda j, i, be: (be[i], 0, j)) if grouped else (lambda j, i: (0, 0, j))
    in_specs = [
        pl.BlockSpec((bm, kdim), lambda j, i, *_: (i, 0)),
        pl.BlockSpec((None, kdim, bn), w_map),
        pl.BlockSpec((None, kdim, bn), w_map),
    ]
    args = [x, wg, wu]
    if grouped:
        in_specs.append(pl.BlockSpec((bm, 1), lambda j, i, *_: (i, 0)))
        args = [block_expert] + args + [row_scale]
    grid_spec = pltpu.PrefetchScalarGridSpec(
        num_scalar_prefetch=1 if grouped else 0,
        grid=(f // bn, rows // bm),
        in_specs=in_specs,
        out_specs=pl.BlockSpec((bm, bn), lambda j, i, *_: (i, j)),
    )
    return pl.pallas_call(
        functools.partial(_glu_kernel, grouped=grouped),
        grid_spec=grid_spec,
        out_shape=jax.ShapeDtypeStruct((rows, f), BF16),
        compiler_params=_params("arbitrary", "arbitrary"),
        name="glu",
    )(*args)


def _head_fold(x, n_heads):
    shift = n_heads
    while shift < LANES:
        x = x + pltpu.roll(x, shift, 1)
        shift *= 2
    return x


def _slab(j):
    return slice(j * LANES, (j + 1) * LANES)


def _rwkv_elem_kernel(t_ref, dec2_ref, iclr2_ref, dec0_ref, iclr0_ref, r_ref, k_ref, v_ref,
                      kk_ref, ka_ref, rk_ref,
                      w0_ref, w1_ref, k0_ref, k1_ref, b0_ref, b1_ref, nkk_ref, rt_ref, bonus_ref,
                      *, rd, ri, n_heads):
    bm, d = r_ref.shape
    nslab = d // LANES
    rep = LANES // n_heads
    n_keys = d // n_heads

    def put_tiled(ref, j, x):
        for i in range(rep):
            piece = x[:, i * n_heads:(i + 1) * n_heads]
            ref[:, _slab(j * rep + i)] = jnp.concatenate([piece] * rep, axis=1)

    tt = t_ref[...]
    tw = [tt[:, i * rd:(i + 1) * rd] for i in range(2)]
    ta = [tt[:, 2 * rd + i * ri:2 * rd + (i + 1) * ri] for i in range(2)]
    outs = ((w0_ref, k0_ref, b0_ref), (w1_ref, k1_ref, b1_ref))
    s_kk = jnp.zeros((bm, LANES), F32)
    s_bonus = jnp.zeros((bm, LANES), F32)
    for j in range(nslab):
        sl = _slab(j)
        k, r = k_ref[:, sl], r_ref[:, sl]
        put_tiled(rt_ref, j, r)
        kk = k * kk_ref[:, sl]
        put_tiled(nkk_ref, j, kk)
        s_kk = s_kk + kk * kk
        for dd, (w_ref, kd_ref, b_ref) in enumerate(outs):
            z = dec0_ref[dd:dd + 1, sl] + _bdot(tw[dd], dec2_ref[dd, :, sl])
            w_log = -(jnp.maximum(-z, 0.0) + jnp.log(1.0 + jnp.exp(-jnp.abs(z)))) - 0.5
            put_tiled(w_ref, j, jnp.exp(-jnp.exp(w_log)))
            a = _sigmoid(iclr0_ref[dd:dd + 1, sl] + _bdot(ta[dd], iclr2_ref[dd, :, sl]))
            k_d = k * (1.0 + (a - 1.0) * ka_ref[:, sl])
            put_tiled(kd_ref, j, k_d)
            put_tiled(b_ref, j, kk * a)
            s_bonus = s_bonus + r * k_d * rk_ref[:, sl]
    inv = lax.rsqrt(jnp.maximum(_head_fold(s_kk, n_heads), 1e-24))
    s_bonus = _head_fold(s_bonus, n_heads)
    for c in range(n_keys):
        sl = _slab(c)
        nkk_ref[:, sl] = -nkk_ref[:, sl] * inv
        b0_ref[:, sl] = b0_ref[:, sl] * inv
        b1_ref[:, sl] = b1_ref[:, sl] * inv
    for j in range(nslab):
        bonus_ref[:, _slab(j)] = s_bonus * v_ref[:, _slab(j)]


def rwkv_elem(tt, dec2, iclr2, dec0, iclr0, r, k, v, kk, ka, rk, *, rd, ri, n_heads):
    t, d = r.shape
    bm = _pick(t, 32, 16)
    tw = tt.shape[1]
    dt = (d // n_heads) * LANES
    tile = pl.BlockSpec((bm, d), lambda i: (i, 0))
    tiled = pl.BlockSpec((bm, dt), lambda i: (i, 0))
    row = pl.BlockSpec((1, d), lambda i: (0, 0))
    return pl.pallas_call(
        functools.partial(_rwkv_elem_kernel, rd=rd, ri=ri, n_heads=n_heads),
        grid=(t // bm,),
        in_specs=[
            pl.BlockSpec((bm, tw), lambda i: (i, 0)),
            pl.BlockSpec((2, rd, d), lambda i: (0, 0, 0)),
            pl.BlockSpec((2, ri, d), lambda i: (0, 0, 0)),
            pl.BlockSpec((2, d), lambda i: (0, 0)),
            pl.BlockSpec((2, d), lambda i: (0, 0)),
            tile, tile, tile, row, row, row,
        ],
        out_specs=[tiled] * 8 + [tile],
        out_shape=[jax.ShapeDtypeStruct((t, dt), F32)] * 8 + [jax.ShapeDtypeStruct((t, d), F32)],
        compiler_params=_params("arbitrary"),
        name="rwkv_elem",
    )(tt, dec2, iclr2, dec0, iclr0, r, k, v, kk, ka, rk)


def _scan_kernel(rf, wf, kf, vf, af, bf, rb, wb, kb, vb, ab, bb, yf_ref, yb_ref,
                 sf_ref, sb_ref, *, tb, n):
    @pl.when(pl.program_id(0) == 0)
    def _():
        sf_ref[...] = jnp.zeros_like(sf_ref)
        sb_ref[...] = jnp.zeros_like(sb_ref)

    nvg = sf_ref.shape[1]
    rows = min(SUBLANES, nvg)
    nvb = nvg // rows

    def vslice(i):
        return pl.ds(i * rows, rows)

    def bcast(ref, t, kc):
        return jnp.broadcast_to(ref[t, pl.ds(kc, 1), :], (rows, LANES))

    def zeros():
        return [jnp.zeros((rows, LANES), F32) for _ in range(nvb)]

    def state_dot_a(t, a_ref, s_ref):
        sa = zeros()
        for kc in range(n):
            a_k = bcast(a_ref, t, kc)
            for i in range(nvb):
                sa[i] = sa[i] + s_ref[kc, vslice(i), :] * a_k
        return tuple(sa)

    def one_direction(t, t_next, r_ref, w_ref, k_ref, v_ref, a_ref, b_ref, y_ref, s_ref, sa):
        vv = [v_ref[t, vslice(i), :] for i in range(nvb)]
        y, sa_next = zeros(), zeros()
        for kc in range(n):
            w_k, b_k, k_k = bcast(w_ref, t, kc), bcast(b_ref, t, kc), bcast(k_ref, t, kc)
            r_k, a_k = bcast(r_ref, t, kc), bcast(a_ref, t_next, kc)
            for i in range(nvb):
                s_new = s_ref[kc, vslice(i), :] * w_k + sa[i] * b_k + vv[i] * k_k
                s_ref[kc, vslice(i), :] = s_new
                y[i] = y[i] + s_new * r_k
                sa_next[i] = sa_next[i] + s_new * a_k
        for i in range(nvb):
            y_ref[t, vslice(i), :] = y[i]
        return tuple(sa_next)

    def step(t, carry):
        sa_f, sa_b = carry
        sa_f = one_direction(t, jnp.minimum(t + 1, tb - 1), rf, wf, kf, vf, af, bf, yf_ref,
                             sf_ref, sa_f)
        u = tb - 1 - t
        sa_b = one_direction(u, jnp.maximum(u - 1, 0), rb, wb, kb, vb, ab, bb, yb_ref,
                             sb_ref, sa_b)
        return sa_f, sa_b

    init = (state_dot_a(0, af, sf_ref), state_dot_a(tb - 1, ab, sb_ref))
    lax.fori_loop(0, tb, step, init)


def rwkv_scan(r, w0, w1, k0, k1, v, a, b0, b1, *, n_lat):
    t, d = v.shape
    n = r.shape[1] // LANES
    nvg = d // LANES
    tb = _pick(math.gcd(n_lat, t - n_lat), 32, 1)
    nb = t // tb
    first = n_lat // tb
    fwd = lambda i: ((i + first) % nb, 0, 0)
    bwd = lambda i: (nb - 1 - i, 0, 0)
    kview = lambda x: x.reshape(t, n, LANES)
    vview = lambda x: x.reshape(t, nvg, LANES)

    def specs(imap):
        ks = pl.BlockSpec((tb, n, LANES), imap)
        vs = pl.BlockSpec((tb, nvg, LANES), imap)
        return [ks, ks, ks, vs, ks, ks], vs

    in_f, out_f = specs(fwd)
    in_b, out_b = specs(bwd)
    y_f, y_b = pl.pallas_call(
        functools.partial(_scan_kernel, tb=tb, n=n),
        grid=(nb,),
        in_specs=in_f + in_b,
        out_specs=[out_f, out_b],
        out_shape=[jax.ShapeDtypeStruct((t, nvg, LANES), F32)] * 2,
        scratch_shapes=[pltpu.VMEM((n, nvg, LANES), F32)] * 2,
        compiler_params=_params("arbitrary"),
        name="rwkv_scan",
    )(kview(r), kview(w0), kview(k0), vview(v), kview(a), kview(b0),
      kview(r), kview(w1), kview(k1), vview(v), kview(a), kview(b1))
    return y_f.reshape(t, d), y_b.reshape(t, d)


def _rwkv_post_kernel(yf_ref, yb_ref, bonus_ref, g_ref, lw_ref, lb_ref, o_ref, *, n_heads):
    bm, d = yf_ref.shape
    nslab = d // LANES
    inv_n = 1.0 / (d // n_heads)
    s1 = jnp.zeros((bm, LANES), F32)
    for j in range(nslab):
        s1 = s1 + (yf_ref[:, _slab(j)] + yb_ref[:, _slab(j)])
    mu = _head_fold(s1, n_heads) * inv_n
    s2 = jnp.zeros((bm, LANES), F32)
    for j in range(nslab):
        yc = yf_ref[:, _slab(j)] + yb_ref[:, _slab(j)] - mu
        s2 = s2 + yc * yc
    rstd = lax.rsqrt(_head_fold(s2, n_heads) * inv_n + LNX_EPS)
    for j in range(nslab):
        sl = _slab(j)
        yn = (yf_ref[:, sl] + yb_ref[:, sl] - mu) * rstd
        out = yn * lw_ref[:, sl] + lb_ref[:, sl] + bonus_ref[:, sl]
        o_ref[:, sl] = (out * g_ref[:, sl].astype(F32)).astype(o_ref.dtype)


def rwkv_post(yf, yb, bonus, g, lnx_w, lnx_b, *, n_heads):
    t, d = yf.shape
    bm = _pick(t, 128, 16)
    tile = pl.BlockSpec((bm, d), lambda i: (i, 0))
    row = pl.BlockSpec((1, d), lambda i: (0, 0))
    return pl.pallas_call(
        functools.partial(_rwkv_post_kernel, n_heads=n_heads),
        grid=(t // bm,),
        in_specs=[tile, tile, tile, tile, row, row],
        out_specs=tile,
        out_shape=jax.ShapeDtypeStruct((t, d), BF16),
        compiler_params=_params("arbitrary"),
        name="rwkv_post",
    )(yf, yb, bonus, g, lnx_w.reshape(1, d), lnx_b.reshape(1, d))


def _qk_epi(scale):
    def epi(acc, ex, i_m):
        gain_ref, cos_ref, sin_ref = ex
        cos, sin = cos_ref[...], sin_ref[...]
        gain = gain_ref[...] * scale
        lane = lax.broadcasted_iota(jnp.int32, cos.shape, 1)
        even = (lane % 2) == 0
        outs = []
        for j in range(acc.shape[1] // HEAD_DIM):
            q = acc[:, j * HEAD_DIM:(j + 1) * HEAD_DIM]
            ms = jnp.mean(q * q, axis=-1, keepdims=True)
            q = q * lax.rsqrt(ms + NORM_EPS) * gain
            partner = jnp.where(even, pltpu.roll(q, HEAD_DIM - 1, 1), pltpu.roll(q, 1, 1))
            outs.append(q * cos + partner * sin)
        return jnp.concatenate(outs, axis=1)
    return epi


def qk_matmul(x, w, gain, cos_t, sin_t, *, rows, scale, lead=(), name="qk_mm"):
    n = w.shape[-1]
    bm = _pick(rows, 1024, 16)
    bn = _pick(n, 512, LANES)
    extras = [
        (gain.reshape(1, HEAD_DIM), (1, HEAD_DIM), lambda j, i, k: (0, 0)),
        (cos_t, (bm, HEAD_DIM), lambda j, i, k: (i, 0)),
        (sin_t, (bm, HEAD_DIM), lambda j, i, k: (i, 0)),
    ]
    return matmul(x, w, rows=rows, lead=lead, out_dtype=BF16, epi=_qk_epi(scale),
                  extras=extras, bm=bm, bn=bn, name=name)


def _attn_kernel(q_ref, k_ref, v_ref, o_ref, *, groups, bk, nchunks):
    bq = q_ref.shape[0]
    q = jnp.concatenate([q_ref[:, g * HEAD_DIM:(g + 1) * HEAD_DIM] for g in range(groups)], axis=0)
    rows = groups * bq

    def body(c, carry):
        m, l, acc = carry
        start = pl.multiple_of(c * bk, bk)
        kc = k_ref[pl.ds(start, bk), :]
        vc = v_ref[pl.ds(start, bk), :]
        s = lax.dot_general(q, kc, (((1,), (1,)), ((), ())), preferred_element_type=F32)
        m_new = jnp.maximum(m, jnp.max(s, axis=1, keepdims=True))
        alpha = jnp.exp2(m - m_new)
        p = jnp.exp2(s - m_new)
        l = alpha * l + jnp.sum(p, axis=1, keepdims=True)
        acc = alpha * acc + _bdot(p.astype(BF16), vc)
        return m_new, l, acc

    init = (jnp.full((rows, 1), -jnp.inf, F32), jnp.zeros((rows, 1), F32),
            jnp.zeros((rows, HEAD_DIM), F32))
    _, l, acc = lax.fori_loop(0, nchunks, body, init)
    out = acc / l
    for g in range(groups):
        o_ref[:, g * HEAD_DIM:(g + 1) * HEAD_DIM] = out[g * bq:(g + 1) * bq].astype(o_ref.dtype)


def attention(q, k, v, *, groups):
    s, dq = q.shape
    tk, dkv = k.shape
    n_kv = dkv // HEAD_DIM
    bq = _pick(s, 256, 16)
    bk = _pick(tk, 2816, 256) if tk % 256 == 0 else tk
    return pl.pallas_call(
        functools.partial(_attn_kernel, groups=groups, bk=bk, nchunks=tk // bk),
        grid=(n_kv, s // bq),
        in_specs=[
            pl.BlockSpec((bq, groups * HEAD_DIM), lambda h, i: (i, h)),
            pl.BlockSpec((tk, HEAD_DIM), lambda h, i: (0, h)),
            pl.BlockSpec((tk, HEAD_DIM), lambda h, i: (0, h)),
        ],
        out_specs=pl.BlockSpec((bq, groups * HEAD_DIM), lambda h, i: (i, h)),
        out_shape=jax.ShapeDtypeStruct((s, dq), BF16),
        compiler_params=_params("arbitrary", "arbitrary"),
        name="attention",
    )(q, k, v)


ROUTE_BLOCK = 256


def _route_plan(idx, wgt, n_experts):
    s = idx.shape[0]
    bm = ROUTE_BLOCK
    p_rows = 2 * s + n_experts * bm
    e_flat = idx.reshape(-1)
    onehot = (e_flat[:, None] == jnp.arange(n_experts, dtype=jnp.int32)[None, :]).astype(jnp.int32)
    counts = jnp.sum(onehot, axis=0)
    padded = ((counts + bm - 1) // bm) * bm
    ends = jnp.cumsum(padded)
    offs = ends - padded
    rank = jnp.cumsum(onehot, axis=0) - onehot
    pos = jnp.sum((offs[None, :] + rank) * onehot, axis=1)
    row_token = jnp.zeros((p_rows,), jnp.int32).at[pos].set(jnp.arange(2 * s, dtype=jnp.int32) // 2)
    row_weight = jnp.zeros((p_rows,), F32).at[pos].set(wgt.reshape(-1))
    blk_start = jnp.arange(p_rows // bm, dtype=jnp.int32) * bm
    block_expert = jnp.minimum(jnp.searchsorted(ends, blk_start, side="right"),
                               n_experts - 1).astype(jnp.int32)
    return pos.reshape(s, 2).astype(jnp.int32), row_token, row_weight.reshape(p_rows, 1), block_expert


def _row_copy(src_hbm, row, dst_vmem, slot, sem):
    return pltpu.make_async_copy(src_hbm.at[row], dst_vmem.at[slot], sem)


def _gather_rows_kernel(tok_ref, x_hbm, o_ref, sem, *, rows_per_step):
    base = pl.program_id(0) * rows_per_step

    def start(i, c):
        _row_copy(x_hbm, tok_ref[base + i], o_ref, i, sem).start()
        return c

    def wait(i, c):
        _row_copy(x_hbm, 0, o_ref, i, sem).wait()
        return c

    lax.fori_loop(0, rows_per_step, start, 0)
    lax.fori_loop(0, rows_per_step, wait, 0)


def gather_rows(x, row_token):
    s, d = x.shape
    p_rows = row_token.shape[0]
    rows_per_step = ROUTE_BLOCK
    x3 = x.reshape(s, d // LANES, LANES)
    grid_spec = pltpu.PrefetchScalarGridSpec(
        num_scalar_prefetch=1,
        grid=(p_rows // rows_per_step,),
        in_specs=[pl.BlockSpec(memory_space=pl.ANY)],
        out_specs=pl.BlockSpec((rows_per_step, d // LANES, LANES), lambda i, tok: (i, 0, 0)),
        scratch_shapes=[pltpu.SemaphoreType.DMA(())],
    )
    out = pl.pallas_call(
        functools.partial(_gather_rows_kernel, rows_per_step=rows_per_step),
        grid_spec=grid_spec,
        out_shape=jax.ShapeDtypeStruct((p_rows, d // LANES, LANES), x.dtype),
        compiler_params=_params("arbitrary"),
        name="moe_gather",
    )(row_token, x3)
    return out.reshape(p_rows, d)


def _combine_kernel(pos_ref, y_hbm, h_ref, g_ref, o_ref, buf, sem, *, rows_per_step):
    base = pl.program_id(0) * rows_per_step

    def start(i, c):
        for slot in range(TOP_K):
            _row_copy(y_hbm, pos_ref[(base + i) * TOP_K + slot], buf.at[slot], i, sem).start()
        return c

    def wait(i, c):
        for slot in range(TOP_K):
            _row_copy(y_hbm, 0, buf.at[slot], i, sem).wait()
        return c

    lax.fori_loop(0, rows_per_step, start, 0)
    lax.fori_loop(0, rows_per_step, wait, 0)
    o_ref[...] = h_ref[...] + g_ref[...] * (buf[0] + buf[1])


def combine_rows(y, pos, h, gate):
    s, d = h.shape
    p_rows = y.shape[0]
    rows_per_step = _pick(s, 128)
    slab = (d // LANES, LANES)
    blk = pl.BlockSpec((rows_per_step,) + slab, lambda i, pos: (i, 0, 0))
    grid_spec = pltpu.PrefetchScalarGridSpec(
        num_scalar_prefetch=1,
        grid=(s // rows_per_step,),
        in_specs=[pl.BlockSpec(memory_space=pl.ANY), blk,
                  pl.BlockSpec(slab, lambda i, pos: (0, 0))],
        out_specs=blk,
        scratch_shapes=[pltpu.VMEM((TOP_K, rows_per_step) + slab, F32),
                        pltpu.SemaphoreType.DMA(())],
    )
    out = pl.pallas_call(
        functools.partial(_combine_kernel, rows_per_step=rows_per_step),
        grid_spec=grid_spec,
        out_shape=jax.ShapeDtypeStruct((s,) + slab, F32),
        compiler_params=_params("arbitrary"),
        name="moe_combine",
    )(pos.reshape(-1), y.reshape((p_rows,) + slab), h.reshape((s,) + slab), gate.reshape(slab))
    return out.reshape(s, d)


def _head_minor_cols(w, n_heads):
    lead = w.shape[:-1]
    n = w.shape[-1] // n_heads
    return jnp.swapaxes(w.reshape(lead + (n_heads, n)), -1, -2).reshape(lead + (n_heads * n,))


def _pad_cols(w, mult):
    pad = (-w.shape[-1]) % mult
    return jnp.pad(w, [(0, 0)] * (w.ndim - 1) + [(0, pad)]) if pad else w


def kernel(x, c, ctx, c_ctx, ada_w, ada_b, norm_gains, rwkv_mix, rwkv_wr, rwkv_wk, rwkv_wv, rwkv_wo, rwkv_dec0, rwkv_dec1, rwkv_dec2, rwkv_iclr0, rwkv_iclr1, rwkv_iclr2, rwkv_gate1, rwkv_gate2, rwkv_kk, rwkv_ka, rwkv_rk, rwkv_lnx_w, rwkv_lnx_b, ffn_w_gate, ffn_w_up, ffn_w_down, attn_wq, attn_wk, attn_wv, attn_wo, attn_q_gain, attn_k_gain, moe_router, moe_w_gate, moe_w_up, moe_w_down, final_norm):
    batch, n_lat, d = x.shape
    n_ctx = ctx.shape[1]
    depth = ada_w.shape[0]
    assert batch == 1 and depth == 2 and rwkv_dec0.shape[1] == 2
    t = n_lat + n_ctx
    n_experts = moe_router.shape[-1]
    assert n_experts <= LANES and TOP_K == 2

    h = jnp.concatenate([x[0], ctx[0]], axis=0)

    cond8 = jnp.zeros((SUBLANES, d), F32).at[0].set(c[0]).at[1].set(c_ctx)
    mod = ada_params(cond8, ada_w, ada_b)[:, :2, :].reshape(depth, 2, N_MOD, d)

    xr, xw, xk, xv, xa, xg = rwkv_prep(h, norm_gains[0, 0], mod[0], rwkv_mix[0], n_lat=n_lat)
    nh = d // RWKV_HEAD
    hm = functools.partial(_head_minor_cols, n_heads=nh)
    pos_id = jnp.arange(d, dtype=jnp.int32)
    src_id = (pos_id % nh) * RWKV_HEAD + pos_id // nh
    perm = (pos_id[:, None] == src_id[None, :]).astype(BF16)
    perm_t = (src_id[:, None] == pos_id[None, :]).astype(BF16)

    def reorder_cols(w):
        return matmul(w, perm, rows=w.shape[0], out_dtype=BF16, bm=512, name="reorder_cols")

    r = matmul(xr, reorder_cols(rwkv_wr[0]), rows=t, name="rwkv_r")
    k = matmul(xk, reorder_cols(rwkv_wk[0]), rows=t, name="rwkv_k")
    v = matmul(xv, reorder_cols(rwkv_wv[0]), rows=t, name="rwkv_v")

    rd, ri = rwkv_dec1.shape[-1], rwkv_iclr1.shape[-1]
    dec1 = jnp.concatenate([rwkv_dec1[0, 0], rwkv_dec1[0, 1]], axis=1)
    iclr1 = jnp.concatenate([rwkv_iclr1[0, 0], rwkv_iclr1[0, 1]], axis=1)
    gate1 = _pad_cols(rwkv_gate1[0], LANES)
    rg = gate1.shape[1]
    pieces = [rwkv_dec2[0].reshape(2 * rd, d), rwkv_iclr2[0].reshape(2 * ri, d), rwkv_gate2[0],
              rwkv_dec0[0], rwkv_iclr0[0], rwkv_kk[0:1], rwkv_ka[0:1], rwkv_rk[0].reshape(1, d),
              rwkv_lnx_w[0:1], rwkv_lnx_b[0:1]]
    stacked = hm(jnp.concatenate(pieces, axis=0))
    bounds = np.cumsum([0] + [p.shape[0] for p in pieces])
    (dec2, iclr2, gate2, dec0, iclr0, kk_w, ka_w, rk_w, lnx_w, lnx_b) = [
        stacked[bounds[i]:bounds[i + 1]] for i in range(len(pieces))]
    gate2 = jnp.pad(gate2, ((0, rg - gate2.shape[0]), (0, 0)))
    tw = matmul(xw, dec1, rows=t, out_dtype=BF16, epi=lambda a, e, i: jnp.tanh(a), name="lora_w")
    ta = matmul(xa, iclr1, rows=t, out_dtype=BF16, name="lora_a")
    tg = matmul(xg, gate1, rows=t, out_dtype=BF16, epi=lambda a, e, i: _sigmoid(a), name="lora_g")
    g = matmul(tg, gate2, rows=t, out_dtype=BF16, name="rwkv_gate")

    tt = jnp.concatenate([tw, ta], axis=1)
    w0, w1, k0, k1, b0, b1, nkk, rt, bonus = rwkv_elem(
        tt, dec2.reshape(2, rd, d).astype(BF16), iclr2.reshape(2, ri, d).astype(BF16), dec0, iclr0,
        r, k, v, kk_w, ka_w, rk_w, rd=rd, ri=ri, n_heads=nh)

    y_f, y_b = rwkv_scan(rt, w0, w1, k0, k1, v, nkk, b0, b1, n_lat=n_lat)
    yo = rwkv_post(y_f, y_b, bonus, g, lnx_w, lnx_b, n_heads=nh)
    wo = matmul(perm_t, rwkv_wo[0], rows=d, out_dtype=BF16, bm=512, name="reorder_rows")
    h = residual_matmul(yo, wo, h, mod[0], rows=t, n_lat=n_lat, gate_idx=2, name="rwkv_out")

    nf = norm_mod(h, norm_gains[0, 1], mod[0], rows=t, n_lat=n_lat, sub=1)
    hid = glu(nf, ffn_w_gate, ffn_w_up, rows=t)
    h = residual_matmul(hid, ffn_w_down.astype(BF16), h, mod[0], rows=t, n_lat=n_lat, gate_idx=5,
                        lead=(0,), bk=_pick(ffn_w_down.shape[1], 5504, LANES), name="ffn_down")

    na = norm_mod(h, norm_gains[1, 0], mod[1], rows=t, n_lat=n_lat, sub=0)
    pairs = HEAD_DIM // 4
    inv_freq = ROPE_THETA ** (-jnp.arange(pairs, dtype=F32) / pairs)
    pos = jnp.arange(n_lat)
    ang = jnp.concatenate([(pos // GRID_W).astype(F32)[:, None] * inv_freq,
                           (pos % GRID_W).astype(F32)[:, None] * inv_freq], axis=-1)
    sign = jnp.tile(jnp.asarray([-1.0, 1.0], F32), HEAD_DIM // 2)
    cos_t = jnp.concatenate([jnp.repeat(jnp.cos(ang), 2, axis=-1),
                             jnp.ones((n_ctx, HEAD_DIM), F32)], axis=0)
    sin_t = jnp.concatenate([jnp.repeat(jnp.sin(ang), 2, axis=-1) * sign,
                             jnp.zeros((n_ctx, HEAD_DIM), F32)], axis=0)
    q = qk_matmul(na, attn_wq, attn_q_gain[0], cos_t, sin_t, rows=n_lat,
                  scale=HEAD_DIM ** -0.5 * math.log2(math.e), lead=(0,), name="attn_q")
    kx = qk_matmul(na, attn_wk, attn_k_gain[0], cos_t, sin_t, rows=t, scale=1.0, lead=(0,),
                   name="attn_k")
    vx = matmul(na, attn_wv, rows=t, lead=(0,), out_dtype=BF16, name="attn_v")
    groups = attn_wq.shape[-1] // attn_wk.shape[-1]
    o = attention(q, kx, vx, groups=groups)
    h = residual_matmul(o, attn_wo, h, mod[1], rows=n_lat, n_lat=n_lat, gate_idx=2, lead=(0,),
                        name="attn_out")

    nm, ridx, rwgt = norm_router(h, norm_gains[1, 1], mod[1, 0], _pad_cols(moe_router[0], LANES),
                                 rows=n_lat, n_experts=n_experts)
    pos, row_token, row_weight, block_expert = _route_plan(ridx[:, :TOP_K], rwgt[:, :TOP_K],
                                                           n_experts)
    p_rows = row_token.shape[0]
    xs = gather_rows(nm, row_token)
    hid = glu(xs, moe_w_gate[0], moe_w_up[0], rows=p_rows, bm=ROUTE_BLOCK, bn=512,
              block_expert=block_expert, row_scale=row_weight)
    ys = matmul(hid, moe_w_down[0], rows=p_rows, block_group=block_expert, bm=ROUTE_BLOCK,
                name="moe_down")
    h = combine_rows(ys, pos, h[:n_lat], mod[1, 0, 5])

    return final_rms_norm(h, final_norm, rows=n_lat)[None]
```

```python
import functools
import math

import numpy as np
import jax
import jax.numpy as jnp
from jax import lax
from jax.experimental import pallas as pl
from jax.experimental.pallas import tpu as pltpu

F32 = jnp.float32
BF16 = jnp.bfloat16

N_MOD = 6
NORM_EPS = 1e-6
LNX_EPS = 64e-5
RWKV_HEAD = 64
HEAD_DIM = 128
GRID_W = 64
ROPE_THETA = 10000.0
TOP_K = 2
LANES = 128
SUBLANES = 8
VMEM_LIMIT = 56 * 1024 * 1024


def _pick(n, target, mult=SUBLANES):
    best = None
    for b in range(mult, min(n, target) + 1, mult):
        if n % b == 0:
            best = b
    return best if best is not None else n


def _params(*sem):
    return pltpu.CompilerParams(dimension_semantics=sem, vmem_limit_bytes=VMEM_LIMIT)


def _sigmoid(x):
    return 1.0 / (1.0 + jnp.exp(-x))


def _silu(x):
    return x * _sigmoid(x)


def _bdot(a, b):
    return jnp.dot(a, b, preferred_element_type=F32)


def _ada_kernel(c_ref, w_ref, b_ref, o_ref):
    s = _silu(c_ref[...]).astype(BF16)
    o_ref[...] = _bdot(s, w_ref[...].astype(BF16)) + b_ref[...]


def ada_params(cond8, ada_w, ada_b):
    depth, d, n = ada_w.shape
    bn = _pick(n, 512, LANES)
    return pl.pallas_call(
        _ada_kernel,
        grid=(depth, n // bn),
        in_specs=[
            pl.BlockSpec((SUBLANES, d), lambda l, j: (0, 0)),
            pl.BlockSpec((None, d, bn), lambda l, j: (l, 0, j)),
            pl.BlockSpec((None, 1, bn), lambda l, j: (l, 0, j)),
        ],
        out_specs=pl.BlockSpec((None, SUBLANES, bn), lambda l, j: (l, 0, j)),
        out_shape=jax.ShapeDtypeStruct((depth, SUBLANES, n), F32),
        compiler_params=_params("arbitrary", "arbitrary"),
        name="ada_params",
    )(cond8, ada_w, ada_b.reshape(depth, 1, n))


def _norm_mod(x, gain, shift, scale):
    ms = jnp.mean(x * x, axis=-1, keepdims=True)
    return (x * lax.rsqrt(ms + NORM_EPS) * gain) * (1.0 + scale) + shift


def _norm_mod_kernel(h_ref, g_ref, m_ref, o_ref, *, sub):
    n = _norm_mod(h_ref[...], g_ref[...], m_ref[3 * sub:3 * sub + 1, :],
                  m_ref[3 * sub + 1:3 * sub + 2, :])
    o_ref[...] = n.astype(o_ref.dtype)


def norm_mod(h, gain, mod, *, rows, n_lat, sub):
    t, d = h.shape
    bm = _pick(math.gcd(n_lat, t - n_lat) if t > n_lat else n_lat, 256)
    nlat_blocks = n_lat // bm
    return pl.pallas_call(
        functools.partial(_norm_mod_kernel, sub=sub),
        grid=(rows // bm,),
        in_specs=[
            pl.BlockSpec((bm, d), lambda i: (i, 0)),
            pl.BlockSpec((1, d), lambda i: (0, 0)),
            pl.BlockSpec((None, N_MOD, d), lambda i: (jnp.where(i >= nlat_blocks, 1, 0), 0, 0)),
        ],
        out_specs=pl.BlockSpec((bm, d), lambda i: (i, 0)),
        out_shape=jax.ShapeDtypeStruct((rows, d), BF16),
        compiler_params=_params("arbitrary"),
        name="norm_mod",
    )(h, gain.reshape(1, d), mod)


def _norm_router_kernel(h_ref, g_ref, m_ref, r_ref, o_ref, idx_ref, wgt_ref, *, n_experts):
    n = _norm_mod(h_ref[...], g_ref[...], m_ref[3:4, :], m_ref[4:5, :])
    o_ref[...] = n.astype(o_ref.dtype)
    logits = jnp.dot(n, r_ref[...], precision=lax.Precision.HIGHEST,
                     preferred_element_type=F32)
    lane = lax.broadcasted_iota(jnp.int32, logits.shape, 1)
    neg = jnp.float32(-jnp.inf)
    lg = jnp.where(lane < n_experts, logits, neg)
    m1 = jnp.max(lg, axis=-1, keepdims=True)
    i1 = jnp.min(jnp.where(lg == m1, lane, LANES), axis=-1, keepdims=True)
    lg2 = jnp.where(lane == i1, neg, lg)
    m2 = jnp.max(lg2, axis=-1, keepdims=True)
    i2 = jnp.min(jnp.where(lg2 == m2, lane, LANES), axis=-1, keepdims=True)
    e = jnp.exp(m2 - m1)
    w1 = 1.0 / (1.0 + e)
    idx_ref[...] = jnp.where(lane == 0, i1, jnp.where(lane == 1, i2, 0))
    wgt_ref[...] = jnp.where(lane == 0, w1, jnp.where(lane == 1, 1.0 - w1, 0.0))


def norm_router(h, gain, mod_lat, router_pad, *, rows, n_experts):
    t, d = h.shape
    bm = _pick(rows, 256)
    lane_blk = pl.BlockSpec((bm, LANES), lambda i: (i, 0))
    return pl.pallas_call(
        functools.partial(_norm_router_kernel, n_experts=n_experts),
        grid=(rows // bm,),
        in_specs=[
            pl.BlockSpec((bm, d), lambda i: (i, 0)),
            pl.BlockSpec((1, d), lambda i: (0, 0)),
            pl.BlockSpec((N_MOD, d), lambda i: (0, 0)),
            pl.BlockSpec((d, LANES), lambda i: (0, 0)),
        ],
        out_specs=[pl.BlockSpec((bm, d), lambda i: (i, 0)), lane_blk, lane_blk],
        out_shape=[jax.ShapeDtypeStruct((rows, d), BF16),
                   jax.ShapeDtypeStruct((rows, LANES), jnp.int32),
                   jax.ShapeDtypeStruct((rows, LANES), F32)],
        compiler_params=_params("arbitrary"),
        name="norm_router",
    )(h, gain.reshape(1, d), mod_lat, router_pad)


def _final_norm_kernel(h_ref, g_ref, o_ref):
    x = h_ref[...]
    ms = jnp.mean(x * x, axis=-1, keepdims=True)
    o_ref[...] = x * lax.rsqrt(ms + NORM_EPS) * g_ref[...]


def final_rms_norm(h, gain, *, rows):
    t, d = h.shape
    bm = _pick(rows, 256)
    return pl.pallas_call(
        _final_norm_kernel,
        grid=(rows // bm,),
        in_specs=[pl.BlockSpec((bm, d), lambda i: (i, 0)),
                  pl.BlockSpec((1, d), lambda i: (0, 0))],
        out_specs=pl.BlockSpec((bm, d), lambda i: (i, 0)),
        out_shape=jax.ShapeDtypeStruct((rows, d), F32),
        compiler_params=_params("arbitrary"),
        name="final_norm",
    )(h, gain.reshape(1, d))


def _rwkv_prep_kernel(h_ref, hp_ref, hn_ref, g_ref, m_ref, mix_ref, *o_refs,
                      nlat_blocks, nblocks):
    i = pl.program_id(0)
    gain, shift, scale = g_ref[...], m_ref[0:1, :], m_ref[1:2, :]
    n = _norm_mod(h_ref[...], gain, shift, scale)
    n_prev = _norm_mod(hp_ref[...], gain, shift, scale)[SUBLANES - 1:SUBLANES, :]
    n_next = _norm_mod(hn_ref[...], gain, shift, scale)[0:1, :]
    first = jnp.logical_or(i == 0, i == nlat_blocks)
    last = jnp.logical_or(i == nlat_blocks - 1, i == nblocks - 1)
    n_prev = jnp.where(first, 0.0, n_prev)
    n_next = jnp.where(last, 0.0, n_next)
    bm = n.shape[0]
    row = lax.broadcasted_iota(jnp.int32, n.shape, 0)
    prev = jnp.where(row == 0, n_prev, pltpu.roll(n, 1, 0))
    nxt = jnp.where(row == bm - 1, n_next, pltpu.roll(n, bm - 1, 0))
    dh = 0.5 * (prev + nxt) - n
    for j, o_ref in enumerate(o_refs):
        o_ref[...] = (n + dh * mix_ref[j:j + 1, :]).astype(o_ref.dtype)


def rwkv_prep(h, gain, mod, mix, *, n_lat):
    t, d = h.shape
    bm = _pick(math.gcd(n_lat, t - n_lat), 128)
    nblocks, nlat_blocks = t // bm, n_lat // bm
    r8 = bm // SUBLANES
    n8 = t // SUBLANES
    return pl.pallas_call(
        functools.partial(_rwkv_prep_kernel, nlat_blocks=nlat_blocks, nblocks=nblocks),
        grid=(nblocks,),
        in_specs=[
            pl.BlockSpec((bm, d), lambda i: (i, 0)),
            pl.BlockSpec((SUBLANES, d), lambda i: (jnp.maximum(i * r8 - 1, 0), 0)),
            pl.BlockSpec((SUBLANES, d), lambda i: (jnp.minimum((i + 1) * r8, n8 - 1), 0)),
            pl.BlockSpec((1, d), lambda i: (0, 0)),
            pl.BlockSpec((None, N_MOD, d), lambda i: (jnp.where(i >= nlat_blocks, 1, 0), 0, 0)),
            pl.BlockSpec((6, d), lambda i: (0, 0)),
        ],
        out_specs=[pl.BlockSpec((bm, d), lambda i: (i, 0))] * 6,
        out_shape=[jax.ShapeDtypeStruct((t, d), BF16)] * 6,
        compiler_params=_params("arbitrary"),
        name="rwkv_prep",
    )(h, h, h, gain.reshape(1, d), mod, mix)


def _mm_kernel(*refs, nk, n_extra, epi, grouped):
    if grouped:
        refs = refs[1:]
    x_ref, w_ref = refs[0], refs[1]
    extras = refs[2:2 + n_extra]
    o_ref = refs[2 + n_extra]
    i_m = pl.program_id(1)
    part = _bdot(x_ref[...].astype(BF16), w_ref[...].astype(BF16))
    if nk == 1:
        o_ref[...] = epi(part, extras, i_m).astype(o_ref.dtype)
    else:
        acc_ref = refs[3 + n_extra]
        k = pl.program_id(2)

        @pl.when(k == 0)
        def _():
            acc_ref[...] = part

        @pl.when(k > 0)
        def _():
            acc_ref[...] += part

        @pl.when(k == nk - 1)
        def _():
            o_ref[...] = epi(acc_ref[...], extras, i_m).astype(o_ref.dtype)


def matmul(x, w, *, rows, lead=(), block_group=None, out_dtype=F32, epi=None, extras=(),
           bm=1056, bn=512, bk=4096, name="matmul"):
    kdim = x.shape[1]
    n = w.shape[-1]
    bm = _pick(rows, bm, 16)
    bn = _pick(n, bn, LANES)
    bk = _pick(kdim, bk, LANES)
    nk = kdim // bk
    if epi is None:
        epi = lambda acc, ex, i_m: acc
    grouped = block_group is not None
    if grouped:
        w_spec = pl.BlockSpec((None, bk, bn), lambda j, i, k, grp: (grp[i], k, j))
    else:
        w_spec = pl.BlockSpec((None,) * len(lead) + (bk, bn),
                              lambda j, i, k: tuple(lead) + (k, j))
    in_specs = [pl.BlockSpec((bm, bk), lambda j, i, k, *_: (i, k)), w_spec]
    args = [x, w]
    for arr, bshape, imap in extras:
        in_specs.append(pl.BlockSpec(bshape, imap))
        args.append(arr)
    if grouped:
        args = [block_group] + args
    grid_spec = pltpu.PrefetchScalarGridSpec(
        num_scalar_prefetch=1 if grouped else 0,
        grid=(n // bn, rows // bm, nk),
        in_specs=in_specs,
        out_specs=pl.BlockSpec((bm, bn), lambda j, i, k, *_: (i, j)),
        scratch_shapes=[pltpu.VMEM((bm, bn), F32)] if nk > 1 else [],
    )
    return pl.pallas_call(
        functools.partial(_mm_kernel, nk=nk, n_extra=len(extras), epi=epi, grouped=grouped),
        grid_spec=grid_spec,
        out_shape=jax.ShapeDtypeStruct((rows, n), out_dtype),
        compiler_params=_params("arbitrary", "arbitrary", "arbitrary"),
        name=name,
    )(*args)


def _residual_epi(n_lat, gate_idx):
    def epi(acc, ex, i_m):
        h_ref, m_ref = ex
        bm = acc.shape[0]
        row = i_m * bm + lax.broadcasted_iota(jnp.int32, (bm, 1), 0)
        gate = jnp.where(row < n_lat, m_ref[0, gate_idx:gate_idx + 1, :],
                         m_ref[1, gate_idx:gate_idx + 1, :])
        return h_ref[...] + gate * acc
    return epi


def residual_matmul(x, w, h, mod, *, rows, n_lat, gate_idx, lead=(), bk=4096, name="residual_mm"):
    n = w.shape[-1]
    bm = _pick(rows, 1056, 16)
    bn = _pick(n, 512, LANES)
    extras = [
        (h, (bm, bn), lambda j, i, k: (i, j)),
        (mod, (2, N_MOD, bn), lambda j, i, k: (0, 0, j)),
    ]
    return matmul(x, w, rows=rows, lead=lead, epi=_residual_epi(n_lat, gate_idx),
                  extras=extras, bm=bm, bn=bn, bk=bk, name=name)


def _glu_kernel(*refs, grouped):
    if grouped:
        _, x_ref, wg_ref, wu_ref, scale_ref, o_ref = refs
    else:
        x_ref, wg_ref, wu_ref, o_ref = refs
    x = x_ref[...]
    g = _bdot(x, wg_ref[...].astype(BF16))
    u = _bdot(x, wu_ref[...].astype(BF16))
    hdn = _silu(g) * u
    if grouped:
        hdn = hdn * scale_ref[...]
    o_ref[...] = hdn.astype(o_ref.dtype)


def glu(x, wg, wu, *, rows, bm=1056, bn=256, block_expert=None, row_scale=None):
    n_exp, kdim, f = wg.shape
    grouped = block_expert is not None
    assert grouped or n_exp == 1
    bm = _pick(rows, bm, 16)
    bn = _pick(f, bn, LANES)
    w_map = (lambda j, i, be: (be[i], 0, j)) if grouped else (lambda j, i: (0, 0, j))
    in_specs = [
        pl.BlockSpec((bm, kdim), lambda j, i, *_: (i, 0)),
        pl.BlockSpec((None, kdim, bn), w_map),
        pl.BlockSpec((None, kdim, bn), w_map),
    ]
    args = [x, wg, wu]
    if grouped:
        in_specs.append(pl.BlockSpec((bm, 1), lambda j, i, *_: (i, 0)))
        args = [block_expert] + args + [row_scale]
    grid_spec = pltpu.PrefetchScalarGridSpec(
        num_scalar_prefetch=1 if grouped else 0,
        grid=(f // bn, rows // bm),
        in_specs=in_specs,
        out_specs=pl.BlockSpec((bm, bn), lambda j, i, *_: (i, j)),
    )
    return pl.pallas_call(
        functools.partial(_glu_kernel, grouped=grouped),
        grid_spec=grid_spec,
        out_shape=jax.ShapeDtypeStruct((rows, f), BF16),
        compiler_params=_params("arbitrary", "arbitrary"),
        name="glu",
    )(*args)


def _head_fold(x, n_heads):
    shift = n_heads
    while shift < LANES:
        x = x + pltpu.roll(x, shift, 1)
        shift *= 2
    return x


def _slab(j):
    return slice(j * LANES, (j + 1) * LANES)


def _rwkv_elem_kernel(t_ref, dec2_ref, iclr2_ref, dec0_ref, iclr0_ref, r_ref, k_ref, v_ref,
                      kk_ref, ka_ref, rk_ref,
                      w0_ref, w1_ref, k0_ref, k1_ref, b0_ref, b1_ref, nkk_ref, rt_ref, bonus_ref,
                      *, rd, ri, n_heads):
    bm, d = r_ref.shape
    nslab = d // LANES
    rep = LANES // n_heads
    n_keys = d // n_heads

    def put_tiled(ref, j, x):
        for i in range(rep):
            piece = x[:, i * n_heads:(i + 1) * n_heads]
            ref[:, _slab(j * rep + i)] = jnp.concatenate([piece] * rep, axis=1)

    tt = t_ref[...]
    tw = [tt[:, i * rd:(i + 1) * rd] for i in range(2)]
    ta = [tt[:, 2 * rd + i * ri:2 * rd + (i + 1) * ri] for i in range(2)]
    outs = ((w0_ref, k0_ref, b0_ref), (w1_ref, k1_ref, b1_ref))
    s_kk = jnp.zeros((bm, LANES), F32)
    s_bonus = jnp.zeros((bm, LANES), F32)
    for j in range(nslab):
        sl = _slab(j)
        k, r = k_ref[:, sl], r_ref[:, sl]
        put_tiled(rt_ref, j, r)
        kk = k * kk_ref[:, sl]
        put_tiled(nkk_ref, j, kk)
        s_kk = s_kk + kk * kk
        for dd, (w_ref, kd_ref, b_ref) in enumerate(outs):
            z = dec0_ref[dd:dd + 1, sl] + _bdot(tw[dd], dec2_ref[dd, :, sl])
            w_log = -(jnp.maximum(-z, 0.0) + jnp.log(1.0 + jnp.exp(-jnp.abs(z)))) - 0.5
            put_tiled(w_ref, j, jnp.exp(-jnp.exp(w_log)))
            a = _sigmoid(iclr0_ref[dd:dd + 1, sl] + _bdot(ta[dd], iclr2_ref[dd, :, sl]))
            k_d = k * (1.0 + (a - 1.0) * ka_ref[:, sl])
            put_tiled(kd_ref, j, k_d)
            put_tiled(b_ref, j, kk * a)
            s_bonus = s_bonus + r * k_d * rk_ref[:, sl]
    inv = lax.rsqrt(jnp.maximum(_head_fold(s_kk, n_heads), 1e-24))
    s_bonus = _head_fold(s_bonus, n_heads)
    for c in range(n_keys):
        sl = _slab(c)
        nkk_ref[:, sl] = -nkk_ref[:, sl] * inv
        b0_ref[:, sl] = b0_ref[:, sl] * inv
        b1_ref[:, sl] = b1_ref[:, sl] * inv
    for j in range(nslab):
        bonus_ref[:, _slab(j)] = s_bonus * v_ref[:, _slab(j)]


def rwkv_elem(tt, dec2, iclr2, dec0, iclr0, r, k, v, kk, ka, rk, *, rd, ri, n_heads):
    t, d = r.shape
    bm = _pick(t, 32, 16)
    tw = tt.shape[1]
    dt = (d // n_heads) * LANES
    tile = pl.BlockSpec((bm, d), lambda i: (i, 0))
    tiled = pl.BlockSpec((bm, dt), lambda i: (i, 0))
    row = pl.BlockSpec((1, d), lambda i: (0, 0))
    return pl.pallas_call(
        functools.partial(_rwkv_elem_kernel, rd=rd, ri=ri, n_heads=n_heads),
        grid=(t // bm,),
        in_specs=[
            pl.BlockSpec((bm, tw), lambda i: (i, 0)),
            pl.BlockSpec((2, rd, d), lambda i: (0, 0, 0)),
            pl.BlockSpec((2, ri, d), lambda i: (0, 0, 0)),
            pl.BlockSpec((2, d), lambda i: (0, 0)),
            pl.BlockSpec((2, d), lambda i: (0, 0)),
            tile, tile, tile, row, row, row,
        ],
        out_specs=[tiled] * 8 + [tile],
        out_shape=[jax.ShapeDtypeStruct((t, dt), F32)] * 8 + [jax.ShapeDtypeStruct((t, d), F32)],
        compiler_params=_params("arbitrary"),
        name="rwkv_elem",
    )(tt, dec2, iclr2, dec0, iclr0, r, k, v, kk, ka, rk)


def _scan_kernel(rf, wf, kf, vf, af, bf, rb, wb, kb, vb, ab, bb, yf_ref, yb_ref,
                 sf_ref, sb_ref, *, tb, n):
    @pl.when(pl.program_id(0) == 0)
    def _():
        sf_ref[...] = jnp.zeros_like(sf_ref)
        sb_ref[...] = jnp.zeros_like(sb_ref)

    nvg = sf_ref.shape[1]
    rows = min(SUBLANES, nvg)
    nvb = nvg // rows

    def vslice(i):
        return pl.ds(i * rows, rows)

    def bcast(ref, t, kc):
        g, s = t
        return jnp.broadcast_to(ref[g, pl.ds(s, 1), pl.ds(kc * LANES, LANES)], (rows, LANES))

    def row(t):
        return t[0] * SUBLANES + t[1]

    def zeros():
        return [jnp.zeros((rows, LANES), F32) for _ in range(nvb)]

    def state_dot_a(t, a_ref, s_ref):
        sa = zeros()
        for kc in range(n):
            a_k = bcast(a_ref, t, kc)
            for i in range(nvb):
                sa[i] = sa[i] + s_ref[kc, vslice(i), :] * a_k
        return tuple(sa)

    def one_direction(t, t_next, r_ref, w_ref, k_ref, v_ref, a_ref, b_ref, y_ref, s_ref, sa):
        vv = [v_ref[row(t), vslice(i), :] for i in range(nvb)]
        y, sa_next = zeros(), zeros()
        for kc in range(n):
            w_k, b_k, k_k = bcast(w_ref, t, kc), bcast(b_ref, t, kc), bcast(k_ref, t, kc)
            r_k, a_k = bcast(r_ref, t, kc), bcast(a_ref, t_next, kc)
            for i in range(nvb):
                s_new = s_ref[kc, vslice(i), :] * w_k + sa[i] * b_k + vv[i] * k_k
                s_ref[kc, vslice(i), :] = s_new
                y[i] = y[i] + s_new * r_k
                sa_next[i] = sa_next[i] + s_new * a_k
        for i in range(nvb):
            y_ref[row(t), vslice(i), :] = y[i]
        return tuple(sa_next)

    ngroups = tb // SUBLANES

    def group(g, carry):
        sa_f, sa_b = carry
        g_next = jnp.minimum(g + 1, ngroups - 1)
        gb = ngroups - 1 - g
        gb_next = jnp.maximum(gb - 1, 0)
        for s in range(SUBLANES):
            t_next = (g, s + 1) if s + 1 < SUBLANES else (g_next, 0)
            sa_f = one_direction((g, s), t_next, rf, wf, kf, vf, af, bf, yf_ref, sf_ref, sa_f)
            u = SUBLANES - 1 - s
            u_next = (gb, u - 1) if u > 0 else (gb_next, SUBLANES - 1)
            sa_b = one_direction((gb, u), u_next, rb, wb, kb, vb, ab, bb, yb_ref, sb_ref, sa_b)
        return sa_f, sa_b

    init = (state_dot_a((0, 0), af, sf_ref),
            state_dot_a((ngroups - 1, SUBLANES - 1), ab, sb_ref))
    lax.fori_loop(0, ngroups, group, init)


def rwkv_scan(r, w0, w1, k0, k1, v, a, b0, b1, *, n_lat):
    t, d = v.shape
    n = r.shape[1] // LANES
    nvg = d // LANES
    tb = _pick(math.gcd(n_lat, t - n_lat), 32, 1)
    nb = t // tb
    first = n_lat // tb
    fwd = lambda i: (i + first) % nb
    bwd = lambda i: nb - 1 - i
    kview = lambda x: x.reshape(t // SUBLANES, SUBLANES, n * LANES)
    vview = lambda x: x.reshape(t, nvg, LANES)

    def specs(blk):
        ks = pl.BlockSpec((tb // SUBLANES, SUBLANES, n * LANES), lambda i: (blk(i), 0, 0))
        vs = pl.BlockSpec((tb, nvg, LANES), lambda i: (blk(i), 0, 0))
        return [ks, ks, ks, vs, ks, ks], vs

    in_f, out_f = specs(fwd)
    in_b, out_b = specs(bwd)
    y_f, y_b = pl.pallas_call(
        functools.partial(_scan_kernel, tb=tb, n=n),
        grid=(nb,),
        in_specs=in_f + in_b,
        out_specs=[out_f, out_b],
        out_shape=[jax.ShapeDtypeStruct((t, nvg, LANES), F32)] * 2,
        scratch_shapes=[pltpu.VMEM((n, nvg, LANES), F32)] * 2,
        compiler_params=_params("arbitrary"),
        name="rwkv_scan",
    )(kview(r), kview(w0), kview(k0), vview(v), kview(a), kview(b0),
      kview(r), kview(w1), kview(k1), vview(v), kview(a), kview(b1))
    return y_f.reshape(t, d), y_b.reshape(t, d)


def _rwkv_post_kernel(yf_ref, yb_ref, bonus_ref, g_ref, lw_ref, lb_ref, o_ref, *, n_heads):
    bm, d = yf_ref.shape
    nslab = d // LANES
    inv_n = 1.0 / (d // n_heads)
    s1 = jnp.zeros((bm, LANES), F32)
    for j in range(nslab):
        s1 = s1 + (yf_ref[:, _slab(j)] + yb_ref[:, _slab(j)])
    mu = _head_fold(s1, n_heads) * inv_n
    s2 = jnp.zeros((bm, LANES), F32)
    for j in range(nslab):
        yc = yf_ref[:, _slab(j)] + yb_ref[:, _slab(j)] - mu
        s2 = s2 + yc * yc
    rstd = lax.rsqrt(_head_fold(s2, n_heads) * inv_n + LNX_EPS)
    for j in range(nslab):
        sl = _slab(j)
        yn = (yf_ref[:, sl] + yb_ref[:, sl] - mu) * rstd
        out = yn * lw_ref[:, sl] + lb_ref[:, sl] + bonus_ref[:, sl]
        o_ref[:, sl] = (out * g_ref[:, sl].astype(F32)).astype(o_ref.dtype)


def rwkv_post(yf, yb, bonus, g, lnx_w, lnx_b, *, n_heads):
    t, d = yf.shape
    bm = _pick(t, 128, 16)
    tile = pl.BlockSpec((bm, d), lambda i: (i, 0))
    row = pl.BlockSpec((1, d), lambda i: (0, 0))
    return pl.pallas_call(
        functools.partial(_rwkv_post_kernel, n_heads=n_heads),
        grid=(t // bm,),
        in_specs=[tile, tile, tile, tile, row, row],
        out_specs=tile,
        out_shape=jax.ShapeDtypeStruct((t, d), BF16),
        compiler_params=_params("arbitrary"),
        name="rwkv_post",
    )(yf, yb, bonus, g, lnx_w.reshape(1, d), lnx_b.reshape(1, d))


def _qk_epi(scale):
    def epi(acc, ex, i_m):
        gain_ref, cos_ref, sin_ref = ex
        cos, sin = cos_ref[...], sin_ref[...]
        gain = gain_ref[...] * scale
        lane = lax.broadcasted_iota(jnp.int32, cos.shape, 1)
        even = (lane % 2) == 0
        outs = []
        for j in range(acc.shape[1] // HEAD_DIM):
            q = acc[:, j * HEAD_DIM:(j + 1) * HEAD_DIM]
            ms = jnp.mean(q * q, axis=-1, keepdims=True)
            q = q * lax.rsqrt(ms + NORM_EPS) * gain
            partner = jnp.where(even, pltpu.roll(q, HEAD_DIM - 1, 1), pltpu.roll(q, 1, 1))
            outs.append(q * cos + partner * sin)
        return jnp.concatenate(outs, axis=1)
    return epi


def qk_matmul(x, w, gain, cos_t, sin_t, *, rows, scale, lead=(), name="qk_mm"):
    n = w.shape[-1]
    bm = _pick(rows, 1024, 16)
    bn = _pick(n, 512, LANES)
    extras = [
        (gain.reshape(1, HEAD_DIM), (1, HEAD_DIM), lambda j, i, k: (0, 0)),
        (cos_t, (bm, HEAD_DIM), lambda j, i, k: (i, 0)),
        (sin_t, (bm, HEAD_DIM), lambda j, i, k: (i, 0)),
    ]
    return matmul(x, w, rows=rows, lead=lead, out_dtype=BF16, epi=_qk_epi(scale),
                  extras=extras, bm=bm, bn=bn, name=name)


def _attn_kernel(q_ref, k_ref, v_ref, o_ref, *, groups, bk, nchunks):
    bq = q_ref.shape[0]
    q = jnp.concatenate([q_ref[:, g * HEAD_DIM:(g + 1) * HEAD_DIM] for g in range(groups)], axis=0)
    rows = groups * bq

    def body(c, carry):
        m, l, acc = carry
        start = pl.multiple_of(c * bk, bk)
        kc = k_ref[pl.ds(start, bk), :]
        vc = v_ref[pl.ds(start, bk), :]
        s = lax.dot_general(q, kc, (((1,), (1,)), ((), ())), preferred_element_type=F32)
        m_new = jnp.maximum(m, jnp.max(s, axis=1, keepdims=True))
        alpha = jnp.exp2(m - m_new)
        p = jnp.exp2(s - m_new)
        l = alpha * l + jnp.sum(p, axis=1, keepdims=True)
        acc = alpha * acc + _bdot(p.astype(BF16), vc)
        return m_new, l, acc

    init = (jnp.full((rows, 1), -jnp.inf, F32), jnp.zeros((rows, 1), F32),
            jnp.zeros((rows, HEAD_DIM), F32))
    _, l, acc = lax.fori_loop(0, nchunks, body, init)
    out = acc / l
    for g in range(groups):
        o_ref[:, g * HEAD_DIM:(g + 1) * HEAD_DIM] = out[g * bq:(g + 1) * bq].astype(o_ref.dtype)


def attention(q, k, v, *, groups):
    s, dq = q.shape
    tk, dkv = k.shape
    n_kv = dkv // HEAD_DIM
    bq = _pick(s, 256, 16)
    bk = _pick(tk, 2816, 256) if tk % 256 == 0 else tk
    return pl.pallas_call(
        functools.partial(_attn_kernel, groups=groups, bk=bk, nchunks=tk // bk),
        grid=(n_kv, s // bq),
        in_specs=[
            pl.BlockSpec((bq, groups * HEAD_DIM), lambda h, i: (i, h)),
            pl.BlockSpec((tk, HEAD_DIM), lambda h, i: (0, h)),
            pl.BlockSpec((tk, HEAD_DIM), lambda h, i: (0, h)),
        ],
        out_specs=pl.BlockSpec((bq, groups * HEAD_DIM), lambda h, i: (i, h)),
        out_shape=jax.ShapeDtypeStruct((s, dq), BF16),
        compiler_params=_params("arbitrary", "arbitrary"),
        name="attention",
    )(q, k, v)


ROUTE_BLOCK = 256


def _route_plan(idx, wgt, n_experts):
    s = idx.shape[0]
    bm = ROUTE_BLOCK
    p_rows = 2 * s + n_experts * bm
    e_flat = idx.reshape(-1)
    onehot = (e_flat[:, None] == jnp.arange(n_experts, dtype=jnp.int32)[None, :]).astype(jnp.int32)
    counts = jnp.sum(onehot, axis=0)
    padded = ((counts + bm - 1) // bm) * bm
    ends = jnp.cumsum(padded)
    offs = ends - padded
    rank = jnp.cumsum(onehot, axis=0) - onehot
    pos = jnp.sum((offs[None, :] + rank) * onehot, axis=1)
    row_token = jnp.zeros((p_rows,), jnp.int32).at[pos].set(jnp.arange(2 * s, dtype=jnp.int32) // 2)
    row_weight = jnp.zeros((p_rows,), F32).at[pos].set(wgt.reshape(-1))
    blk_start = jnp.arange(p_rows // bm, dtype=jnp.int32) * bm
    block_expert = jnp.minimum(jnp.searchsorted(ends, blk_start, side="right"),
                               n_experts - 1).astype(jnp.int32)
    return pos.reshape(s, 2).astype(jnp.int32), row_token, row_weight.reshape(p_rows, 1), block_expert


def _row_copy(src_hbm, row, dst_vmem, slot, sem):
    return pltpu.make_async_copy(src_hbm.at[row], dst_vmem.at[slot], sem)


def _gather_rows_kernel(tok_ref, x_hbm, o_ref, sem, *, rows_per_step):
    base = pl.program_id(0) * rows_per_step

    def start(i, c):
        _row_copy(x_hbm, tok_ref[base + i], o_ref, i, sem).start()
        return c

    def wait(i, c):
        _row_copy(x_hbm, 0, o_ref, i, sem).wait()
        return c

    lax.fori_loop(0, rows_per_step, start, 0)
    lax.fori_loop(0, rows_per_step, wait, 0)


def gather_rows(x, row_token):
    s, d = x.shape
    p_rows = row_token.shape[0]
    rows_per_step = ROUTE_BLOCK
    x3 = x.reshape(s, d // LANES, LANES)
    grid_spec = pltpu.PrefetchScalarGridSpec(
        num_scalar_prefetch=1,
        grid=(p_rows // rows_per_step,),
        in_specs=[pl.BlockSpec(memory_space=pl.ANY)],
        out_specs=pl.BlockSpec((rows_per_step, d // LANES, LANES), lambda i, tok: (i, 0, 0)),
        scratch_shapes=[pltpu.SemaphoreType.DMA(())],
    )
    out = pl.pallas_call(
        functools.partial(_gather_rows_kernel, rows_per_step=rows_per_step),
        grid_spec=grid_spec,
        out_shape=jax.ShapeDtypeStruct((p_rows, d // LANES, LANES), x.dtype),
        compiler_params=_params("arbitrary"),
        name="moe_gather",
    )(row_token, x3)
    return out.reshape(p_rows, d)


def _combine_kernel(pos_ref, y_hbm, h_ref, g_ref, o_ref, buf, sem, *, rows_per_step):
    base = pl.program_id(0) * rows_per_step

    def start(i, c):
        for slot in range(TOP_K):
            _row_copy(y_hbm, pos_ref[(base + i) * TOP_K + slot], buf.at[slot], i, sem).start()
        return c

    def wait(i, c):
        for slot in range(TOP_K):
            _row_copy(y_hbm, 0, buf.at[slot], i, sem).wait()
        return c

    lax.fori_loop(0, rows_per_step, start, 0)
    lax.fori_loop(0, rows_per_step, wait, 0)
    o_ref[...] = h_ref[...] + g_ref[...] * (buf[0] + buf[1])


def combine_rows(y, pos, h, gate):
    s, d = h.shape
    p_rows = y.shape[0]
    rows_per_step = _pick(s, 128)
    slab = (d // LANES, LANES)
    blk = pl.BlockSpec((rows_per_step,) + slab, lambda i, pos: (i, 0, 0))
    grid_spec = pltpu.PrefetchScalarGridSpec(
        num_scalar_prefetch=1,
        grid=(s // rows_per_step,),
        in_specs=[pl.BlockSpec(memory_space=pl.ANY), blk,
                  pl.BlockSpec(slab, lambda i, pos: (0, 0))],
        out_specs=blk,
        scratch_shapes=[pltpu.VMEM((TOP_K, rows_per_step) + slab, F32),
                        pltpu.SemaphoreType.DMA(())],
    )
    out = pl.pallas_call(
        functools.partial(_combine_kernel, rows_per_step=rows_per_step),
        grid_spec=grid_spec,
        out_shape=jax.ShapeDtypeStruct((s,) + slab, F32),
        compiler_params=_params("arbitrary"),
        name="moe_combine",
    )(pos.reshape(-1), y.reshape((p_rows,) + slab), h.reshape((s,) + slab), gate.reshape(slab))
    return out.reshape(s, d)


def _head_minor_cols(w, n_heads):
    lead = w.shape[:-1]
    n = w.shape[-1] // n_heads
    return jnp.swapaxes(w.reshape(lead + (n_heads, n)), -1, -2).reshape(lead + (n_heads * n,))


def _head_minor_rows(w, n_heads):
    rows, cols = w.shape
    return jnp.swapaxes(w.reshape(n_heads, rows // n_heads, cols), 0, 1).reshape(rows, cols)


def _pad_cols(w, mult):
    pad = (-w.shape[-1]) % mult
    return jnp.pad(w, [(0, 0)] * (w.ndim - 1) + [(0, pad)]) if pad else w


def kernel(x, c, ctx, c_ctx, ada_w, ada_b, norm_gains, rwkv_mix, rwkv_wr, rwkv_wk, rwkv_wv, rwkv_wo, rwkv_dec0, rwkv_dec1, rwkv_dec2, rwkv_iclr0, rwkv_iclr1, rwkv_iclr2, rwkv_gate1, rwkv_gate2, rwkv_kk, rwkv_ka, rwkv_rk, rwkv_lnx_w, rwkv_lnx_b, ffn_w_gate, ffn_w_up, ffn_w_down, attn_wq, attn_wk, attn_wv, attn_wo, attn_q_gain, attn_k_gain, moe_router, moe_w_gate, moe_w_up, moe_w_down, final_norm):
    batch, n_lat, d = x.shape
    n_ctx = ctx.shape[1]
    depth = ada_w.shape[0]
    assert batch == 1 and depth == 2 and rwkv_dec0.shape[1] == 2
    t = n_lat + n_ctx
    n_experts = moe_router.shape[-1]
    assert n_experts <= LANES and TOP_K == 2

    h = jnp.concatenate([x[0], ctx[0]], axis=0)

    cond8 = jnp.zeros((SUBLANES, d), F32).at[0].set(c[0]).at[1].set(c_ctx)
    mod = ada_params(cond8, ada_w, ada_b)[:, :2, :].reshape(depth, 2, N_MOD, d)

    xr, xw, xk, xv, xa, xg = rwkv_prep(h, norm_gains[0, 0], mod[0], rwkv_mix[0], n_lat=n_lat)
    nh = d // RWKV_HEAD
    hm = functools.partial(_head_minor_cols, n_heads=nh)
    r = matmul(xr, hm(rwkv_wr[0]).astype(BF16), rows=t, name="rwkv_r")
    k = matmul(xk, hm(rwkv_wk[0]).astype(BF16), rows=t, name="rwkv_k")
    v = matmul(xv, hm(rwkv_wv[0]).astype(BF16), rows=t, name="rwkv_v")

    rd, ri = rwkv_dec1.shape[-1], rwkv_iclr1.shape[-1]
    dec1 = jnp.concatenate([rwkv_dec1[0, 0], rwkv_dec1[0, 1]], axis=1)
    iclr1 = jnp.concatenate([rwkv_iclr1[0, 0], rwkv_iclr1[0, 1]], axis=1)
    gate1 = _pad_cols(rwkv_gate1[0], LANES)
    rg = gate1.shape[1]
    pieces = [rwkv_dec2[0].reshape(2 * rd, d), rwkv_iclr2[0].reshape(2 * ri, d), rwkv_gate2[0],
              rwkv_dec0[0], rwkv_iclr0[0], rwkv_kk[0:1], rwkv_ka[0:1], rwkv_rk[0].reshape(1, d),
              rwkv_lnx_w[0:1], rwkv_lnx_b[0:1]]
    stacked = hm(jnp.concatenate(pieces, axis=0))
    bounds = np.cumsum([0] + [p.shape[0] for p in pieces])
    (dec2, iclr2, gate2, dec0, iclr0, kk_w, ka_w, rk_w, lnx_w, lnx_b) = [
        stacked[bounds[i]:bounds[i + 1]] for i in range(len(pieces))]
    gate2 = jnp.pad(gate2, ((0, rg - gate2.shape[0]), (0, 0)))
    tw = matmul(xw, dec1, rows=t, out_dtype=BF16, epi=lambda a, e, i: jnp.tanh(a), name="lora_w")
    ta = matmul(xa, iclr1, rows=t, out_dtype=BF16, name="lora_a")
    tg = matmul(xg, gate1, rows=t, out_dtype=BF16, epi=lambda a, e, i: _sigmoid(a), name="lora_g")
    g = matmul(tg, gate2, rows=t, out_dtype=BF16, name="rwkv_gate")

    tt = jnp.concatenate([tw, ta], axis=1)
    w0, w1, k0, k1, b0, b1, nkk, rt, bonus = rwkv_elem(
        tt, dec2.reshape(2, rd, d).astype(BF16), iclr2.reshape(2, ri, d).astype(BF16), dec0, iclr0,
        r, k, v, kk_w, ka_w, rk_w, rd=rd, ri=ri, n_heads=nh)

    y_f, y_b = rwkv_scan(rt, w0, w1, k0, k1, v, nkk, b0, b1, n_lat=n_lat)
    yo = rwkv_post(y_f, y_b, bonus, g, lnx_w, lnx_b, n_heads=nh)
    h = residual_matmul(yo, _head_minor_rows(rwkv_wo[0], nh).astype(BF16), h, mod[0], rows=t,
                        n_lat=n_lat, gate_idx=2, name="rwkv_out")

    nf = norm_mod(h, norm_gains[0, 1], mod[0], rows=t, n_lat=n_lat, sub=1)
    hid = glu(nf, ffn_w_gate, ffn_w_up, rows=t)
    h = residual_matmul(hid, ffn_w_down.astype(BF16), h, mod[0], rows=t, n_lat=n_lat, gate_idx=5,
                        lead=(0,), bk=_pick(ffn_w_down.shape[1], 5504, LANES), name="ffn_down")

    na = norm_mod(h, norm_gains[1, 0], mod[1], rows=t, n_lat=n_lat, sub=0)
    pairs = HEAD_DIM // 4
    inv_freq = ROPE_THETA ** (-jnp.arange(pairs, dtype=F32) / pairs)
    pos = jnp.arange(n_lat)
    ang = jnp.concatenate([(pos // GRID_W).astype(F32)[:, None] * inv_freq,
                           (pos % GRID_W).astype(F32)[:, None] * inv_freq], axis=-1)
    sign = jnp.tile(jnp.asarray([-1.0, 1.0], F32), HEAD_DIM // 2)
    cos_t = jnp.concatenate([jnp.repeat(jnp.cos(ang), 2, axis=-1),
                             jnp.ones((n_ctx, HEAD_DIM), F32)], axis=0)
    sin_t = jnp.concatenate([jnp.repeat(jnp.sin(ang), 2, axis=-1) * sign,
                             jnp.zeros((n_ctx, HEAD_DIM), F32)], axis=0)
    q = qk_matmul(na, attn_wq, attn_q_gain[0], cos_t, sin_t, rows=n_lat,
                  scale=HEAD_DIM ** -0.5 * math.log2(math.e), lead=(0,), name="attn_q")
    kx = qk_matmul(na, attn_wk, attn_k_gain[0], cos_t, sin_t, rows=t, scale=1.0, lead=(0,),
                   name="attn_k")
    vx = matmul(na, attn_wv, rows=t, lead=(0,), out_dtype=BF16, name="attn_v")
    groups = attn_wq.shape[-1] // attn_wk.shape[-1]
    o = attention(q, kx, vx, groups=groups)
    h = residual_matmul(o, attn_wo, h, mod[1], rows=n_lat, n_lat=n_lat, gate_idx=2, lead=(0,),
                        name="attn_out")

    nm, ridx, rwgt = norm_router(h, norm_gains[1, 1], mod[1, 0], _pad_cols(moe_router[0], LANES),
                                 rows=n_lat, n_experts=n_experts)
    pos, row_token, row_weight, block_expert = _route_plan(ridx[:, :TOP_K], rwgt[:, :TOP_K],
                                                           n_experts)
    p_rows = row_token.shape[0]
    xs = gather_rows(nm, row_token)
    hid = glu(xs, moe_w_gate[0], moe_w_up[0], rows=p_rows, bm=ROUTE_BLOCK, bn=512,
              block_expert=block_expert, row_scale=row_weight)
    ys = matmul(hid, moe_w_down[0], rows=p_rows, block_group=block_expert, bm=ROUTE_BLOCK,
                name="moe_down")
    h = combine_rows(ys, pos, h[:n_lat], mod[1, 0, 5])

    return final_rms_norm(h, final_norm, rows=n_lat)[None]
```

```python
import functools
import math

import numpy as np
import jax
import jax.numpy as jnp
from jax import lax
from jax.experimental import pallas as pl
from jax.experimental.pallas import tpu as pltpu

F32 = jnp.float32
BF16 = jnp.bfloat16

N_MOD = 6
NORM_EPS = 1e-6
LNX_EPS = 64e-5
RWKV_HEAD = 64
HEAD_DIM = 128
GRID_W = 64
ROPE_THETA = 10000.0
TOP_K = 2
LANES = 128
SUBLANES = 8
VMEM_LIMIT = 56 * 1024 * 1024


def _pick(n, target, mult=SUBLANES):
    best = None
    for b in range(mult, min(n, target) + 1, mult):
        if n % b == 0:
            best = b
    return best if best is not None else n


def _params(*sem):
    return pltpu.CompilerParams(dimension_semantics=sem, vmem_limit_bytes=VMEM_LIMIT)


def _sigmoid(x):
    return 1.0 / (1.0 + jnp.exp(-x))


def _silu(x):
    return x * _sigmoid(x)


def _bdot(a, b):
    return jnp.dot(a, b, preferred_element_type=F32)


def _ada_kernel(c_ref, w_ref, b_ref, o_ref):
    s = _silu(c_ref[...]).astype(BF16)
    o_ref[...] = _bdot(s, w_ref[...].astype(BF16)) + b_ref[...]


def ada_params(cond8, ada_w, ada_b):
    depth, d, n = ada_w.shape
    bn = _pick(n, 512, LANES)
    return pl.pallas_call(
        _ada_kernel,
        grid=(depth, n // bn),
        in_specs=[
            pl.BlockSpec((SUBLANES, d), lambda l, j: (0, 0)),
            pl.BlockSpec((None, d, bn), lambda l, j: (l, 0, j)),
            pl.BlockSpec((None, 1, bn), lambda l, j: (l, 0, j)),
        ],
        out_specs=pl.BlockSpec((None, SUBLANES, bn), lambda l, j: (l, 0, j)),
        out_shape=jax.ShapeDtypeStruct((depth, SUBLANES, n), F32),
        compiler_params=_params("arbitrary", "arbitrary"),
        name="ada_params",
    )(cond8, ada_w, ada_b.reshape(depth, 1, n))


def _norm_mod(x, gain, shift, scale):
    ms = jnp.mean(x * x, axis=-1, keepdims=True)
    return (x * lax.rsqrt(ms + NORM_EPS) * gain) * (1.0 + scale) + shift


def _norm_mod_kernel(h_ref, g_ref, m_ref, o_ref, *, sub):
    n = _norm_mod(h_ref[...], g_ref[...], m_ref[3 * sub:3 * sub + 1, :],
                  m_ref[3 * sub + 1:3 * sub + 2, :])
    o_ref[...] = n.astype(o_ref.dtype)


def norm_mod(h, gain, mod, *, rows, n_lat, sub):
    t, d = h.shape
    bm = _pick(math.gcd(n_lat, t - n_lat) if t > n_lat else n_lat, 256)
    nlat_blocks = n_lat // bm
    return pl.pallas_call(
        functools.partial(_norm_mod_kernel, sub=sub),
        grid=(rows // bm,),
        in_specs=[
            pl.BlockSpec((bm, d), lambda i: (i, 0)),
            pl.BlockSpec((1, d), lambda i: (0, 0)),
            pl.BlockSpec((None, N_MOD, d), lambda i: (jnp.where(i >= nlat_blocks, 1, 0), 0, 0)),
        ],
        out_specs=pl.BlockSpec((bm, d), lambda i: (i, 0)),
        out_shape=jax.ShapeDtypeStruct((rows, d), BF16),
        compiler_params=_params("arbitrary"),
        name="norm_mod",
    )(h, gain.reshape(1, d), mod)


def _norm_router_kernel(h_ref, g_ref, m_ref, r_ref, o_ref, idx_ref, wgt_ref, *, n_experts):
    n = _norm_mod(h_ref[...], g_ref[...], m_ref[3:4, :], m_ref[4:5, :])
    o_ref[...] = n.astype(o_ref.dtype)
    logits = jnp.dot(n, r_ref[...], precision=lax.Precision.HIGHEST,
                     preferred_element_type=F32)
    lane = lax.broadcasted_iota(jnp.int32, logits.shape, 1)
    neg = jnp.float32(-jnp.inf)
    lg = jnp.where(lane < n_experts, logits, neg)
    m1 = jnp.max(lg, axis=-1, keepdims=True)
    i1 = jnp.min(jnp.where(lg == m1, lane, LANES), axis=-1, keepdims=True)
    lg2 = jnp.where(lane == i1, neg, lg)
    m2 = jnp.max(lg2, axis=-1, keepdims=True)
    i2 = jnp.min(jnp.where(lg2 == m2, lane, LANES), axis=-1, keepdims=True)
    e = jnp.exp(m2 - m1)
    w1 = 1.0 / (1.0 + e)
    idx_ref[...] = jnp.where(lane == 0, i1, jnp.where(lane == 1, i2, 0))
    wgt_ref[...] = jnp.where(lane == 0, w1, jnp.where(lane == 1, 1.0 - w1, 0.0))


def norm_router(h, gain, mod_lat, router_pad, *, rows, n_experts):
    t, d = h.shape
    bm = _pick(rows, 256)
    lane_blk = pl.BlockSpec((bm, LANES), lambda i: (i, 0))
    return pl.pallas_call(
        functools.partial(_norm_router_kernel, n_experts=n_experts),
        grid=(rows // bm,),
        in_specs=[
            pl.BlockSpec((bm, d), lambda i: (i, 0)),
            pl.BlockSpec((1, d), lambda i: (0, 0)),
            pl.BlockSpec((N_MOD, d), lambda i: (0, 0)),
            pl.BlockSpec((d, LANES), lambda i: (0, 0)),
        ],
        out_specs=[pl.BlockSpec((bm, d), lambda i: (i, 0)), lane_blk, lane_blk],
        out_shape=[jax.ShapeDtypeStruct((rows, d), BF16),
                   jax.ShapeDtypeStruct((rows, LANES), jnp.int32),
                   jax.ShapeDtypeStruct((rows, LANES), F32)],
        compiler_params=_params("arbitrary"),
        name="norm_router",
    )(h, gain.reshape(1, d), mod_lat, router_pad)


def _rwkv_prep_kernel(h_ref, hp_ref, hn_ref, g_ref, m_ref, mix_ref, *o_refs,
                      nlat_blocks, nblocks):
    i = pl.program_id(0)
    gain, shift, scale = g_ref[...], m_ref[0:1, :], m_ref[1:2, :]
    n = _norm_mod(h_ref[...], gain, shift, scale)
    n_prev = _norm_mod(hp_ref[...], gain, shift, scale)[SUBLANES - 1:SUBLANES, :]
    n_next = _norm_mod(hn_ref[...], gain, shift, scale)[0:1, :]
    first = jnp.logical_or(i == 0, i == nlat_blocks)
    last = jnp.logical_or(i == nlat_blocks - 1, i == nblocks - 1)
    n_prev = jnp.where(first, 0.0, n_prev)
    n_next = jnp.where(last, 0.0, n_next)
    bm = n.shape[0]
    row = lax.broadcasted_iota(jnp.int32, n.shape, 0)
    prev = jnp.where(row == 0, n_prev, pltpu.roll(n, 1, 0))
    nxt = jnp.where(row == bm - 1, n_next, pltpu.roll(n, bm - 1, 0))
    dh = 0.5 * (prev + nxt) - n
    for j, o_ref in enumerate(o_refs):
        o_ref[...] = (n + dh * mix_ref[j:j + 1, :]).astype(o_ref.dtype)


def rwkv_prep(h, gain, mod, mix, *, n_lat):
    t, d = h.shape
    bm = _pick(math.gcd(n_lat, t - n_lat), 128)
    nblocks, nlat_blocks = t // bm, n_lat // bm
    r8 = bm // SUBLANES
    n8 = t // SUBLANES
    return pl.pallas_call(
        functools.partial(_rwkv_prep_kernel, nlat_blocks=nlat_blocks, nblocks=nblocks),
        grid=(nblocks,),
        in_specs=[
            pl.BlockSpec((bm, d), lambda i: (i, 0)),
            pl.BlockSpec((SUBLANES, d), lambda i: (jnp.maximum(i * r8 - 1, 0), 0)),
            pl.BlockSpec((SUBLANES, d), lambda i: (jnp.minimum((i + 1) * r8, n8 - 1), 0)),
            pl.BlockSpec((1, d), lambda i: (0, 0)),
            pl.BlockSpec((None, N_MOD, d), lambda i: (jnp.where(i >= nlat_blocks, 1, 0), 0, 0)),
            pl.BlockSpec((6, d), lambda i: (0, 0)),
        ],
        out_specs=[pl.BlockSpec((bm, d), lambda i: (i, 0))] * 6,
        out_shape=[jax.ShapeDtypeStruct((t, d), BF16)] * 6,
        compiler_params=_params("arbitrary"),
        name="rwkv_prep",
    )(h, h, h, gain.reshape(1, d), mod, mix)


def _mm_kernel(*refs, nk, n_extra, epi, grouped):
    if grouped:
        refs = refs[1:]
    x_ref, w_ref = refs[0], refs[1]
    extras = refs[2:2 + n_extra]
    o_ref = refs[2 + n_extra]
    i_m = pl.program_id(1)
    part = _bdot(x_ref[...].astype(BF16), w_ref[...].astype(BF16))
    if nk == 1:
        o_ref[...] = epi(part, extras, i_m).astype(o_ref.dtype)
    else:
        acc_ref = refs[3 + n_extra]
        k = pl.program_id(2)

        @pl.when(k == 0)
        def _():
            acc_ref[...] = part

        @pl.when(k > 0)
        def _():
            acc_ref[...] += part

        @pl.when(k == nk - 1)
        def _():
            o_ref[...] = epi(acc_ref[...], extras, i_m).astype(o_ref.dtype)


def matmul(x, w, *, rows, lead=(), block_group=None, out_dtype=F32, epi=None, extras=(),
           bm=1056, bn=512, bk=4096, name="matmul"):
    kdim = x.shape[1]
    n = w.shape[-1]
    bm = _pick(rows, bm, 16)
    bn = _pick(n, bn, LANES)
    bk = _pick(kdim, bk, LANES)
    nk = kdim // bk
    if epi is None:
        epi = lambda acc, ex, i_m: acc
    grouped = block_group is not None
    if grouped:
        w_spec = pl.BlockSpec((None, bk, bn), lambda j, i, k, grp: (grp[i], k, j))
    else:
        w_spec = pl.BlockSpec((None,) * len(lead) + (bk, bn),
                              lambda j, i, k: tuple(lead) + (k, j))
    in_specs = [pl.BlockSpec((bm, bk), lambda j, i, k, *_: (i, k)), w_spec]
    args = [x, w]
    for arr, bshape, imap in extras:
        in_specs.append(pl.BlockSpec(bshape, imap))
        args.append(arr)
    if grouped:
        args = [block_group] + args
    grid_spec = pltpu.PrefetchScalarGridSpec(
        num_scalar_prefetch=1 if grouped else 0,
        grid=(n // bn, rows // bm, nk),
        in_specs=in_specs,
        out_specs=pl.BlockSpec((bm, bn), lambda j, i, k, *_: (i, j)),
        scratch_shapes=[pltpu.VMEM((bm, bn), F32)] if nk > 1 else [],
    )
    return pl.pallas_call(
        functools.partial(_mm_kernel, nk=nk, n_extra=len(extras), epi=epi, grouped=grouped),
        grid_spec=grid_spec,
        out_shape=jax.ShapeDtypeStruct((rows, n), out_dtype),
        compiler_params=_params("arbitrary", "arbitrary", "arbitrary"),
        name=name,
    )(*args)


def _residual_epi(n_lat, gate_idx):
    def epi(acc, ex, i_m):
        h_ref, m_ref = ex
        bm = acc.shape[0]
        row = i_m * bm + lax.broadcasted_iota(jnp.int32, (bm, 1), 0)
        gate = jnp.where(row < n_lat, m_ref[0, gate_idx:gate_idx + 1, :],
                         m_ref[1, gate_idx:gate_idx + 1, :])
        return h_ref[...] + gate * acc
    return epi


def residual_matmul(x, w, h, mod, *, rows, n_lat, gate_idx, lead=(), bk=4096, name="residual_mm"):
    n = w.shape[-1]
    bm = _pick(rows, 1056, 16)
    bn = _pick(n, 512, LANES)
    extras = [
        (h, (bm, bn), lambda j, i, k: (i, j)),
        (mod, (2, N_MOD, bn), lambda j, i, k: (0, 0, j)),
    ]
    return matmul(x, w, rows=rows, lead=lead, epi=_residual_epi(n_lat, gate_idx),
                  extras=extras, bm=bm, bn=bn, bk=bk, name=name)


def _glu_kernel(*refs, grouped):
    if grouped:
        _, x_ref, wg_ref, wu_ref, scale_ref, o_ref = refs
    else:
        x_ref, wg_ref, wu_ref, o_ref = refs
    x = x_ref[...]
    g = _bdot(x, wg_ref[...].astype(BF16))
    u = _bdot(x, wu_ref[...].astype(BF16))
    hdn = _silu(g) * u
    if grouped:
        hdn = hdn * scale_ref[...]
    o_ref[...] = hdn.astype(o_ref.dtype)


def glu(x, wg, wu, *, rows, bm=1056, bn=256, block_expert=None, row_scale=None):
    n_exp, kdim, f = wg.shape
    grouped = block_expert is not None
    assert grouped or n_exp == 1
    bm = _pick(rows, bm, 16)
    bn = _pick(f, bn, LANES)
    w_map = (lambda j, i, be: (be[i], 0, j)) if grouped else (lambda j, i: (0, 0, j))
    in_specs = [
        pl.BlockSpec((bm, kdim), lambda j, i, *_: (i, 0)),
        pl.BlockSpec((None, kdim, bn), w_map),
        pl.BlockSpec((None, kdim, bn), w_map),
    ]
    args = [x, wg, wu]
    if grouped:
        in_specs.append(pl.BlockSpec((bm, 1), lambda j, i, *_: (i, 0)))
        args = [block_expert] + args + [row_scale]
    grid_spec = pltpu.PrefetchScalarGridSpec(
        num_scalar_prefetch=1 if grouped else 0,
        grid=(f // bn, rows // bm),
        in_specs=in_specs,
        out_specs=pl.BlockSpec((bm, bn), lambda j, i, *_: (i, j)),
    )
    return pl.pallas_call(
        functools.partial(_glu_kernel, grouped=grouped),
        grid_spec=grid_spec,
        out_shape=jax.ShapeDtypeStruct((rows, f), BF16),
        compiler_params=_params("arbitrary", "arbitrary"),
        name="glu",
    )(*args)


def _head_fold(x, n_heads):
    shift = n_heads
    while shift < LANES:
        x = x + pltpu.roll(x, shift, 1)
        shift *= 2
    return x


def _slab(j):
    return slice(j * LANES, (j + 1) * LANES)


def _rwkv_elem_kernel(t_ref, dec2_ref, iclr2_ref, dec0_ref, iclr0_ref, r_ref, k_ref, v_ref,
                      kk_ref, ka_ref, rk_ref,
                      w0_ref, w1_ref, k0_ref, k1_ref, b0_ref, b1_ref, nkk_ref, rt_ref, bonus_ref,
                      *, rd, ri, n_heads):
    bm, d = r_ref.shape
    nslab = d // LANES
    rep = LANES // n_heads
    n_keys = d // n_heads

    def put_tiled(ref, j, x):
        for i in range(rep):
            piece = x[:, i * n_heads:(i + 1) * n_heads]
            ref[:, _slab(j * rep + i)] = jnp.concatenate([piece] * rep, axis=1)

    tt = t_ref[...]
    tw = [tt[:, i * rd:(i + 1) * rd] for i in range(2)]
    ta = [tt[:, 2 * rd + i * ri:2 * rd + (i + 1) * ri] for i in range(2)]
    outs = ((w0_ref, k0_ref, b0_ref), (w1_ref, k1_ref, b1_ref))
    s_kk = jnp.zeros((bm, LANES), F32)
    s_bonus = jnp.zeros((bm, LANES), F32)
    for j in range(nslab):
        sl = _slab(j)
        k, r = k_ref[:, sl], r_ref[:, sl]
        put_tiled(rt_ref, j, r)
        kk = k * kk_ref[:, sl]
        put_tiled(nkk_ref, j, kk)
        s_kk = s_kk + kk * kk
        for dd, (w_ref, kd_ref, b_ref) in enumerate(outs):
            z = dec0_ref[dd:dd + 1, sl] + _bdot(tw[dd], dec2_ref[dd, :, sl])
            w_log = -(jnp.maximum(-z, 0.0) + jnp.log(1.0 + jnp.exp(-jnp.abs(z)))) - 0.5
            put_tiled(w_ref, j, jnp.exp(-jnp.exp(w_log)))
            a = _sigmoid(iclr0_ref[dd:dd + 1, sl] + _bdot(ta[dd], iclr2_ref[dd, :, sl]))
            k_d = k * (1.0 + (a - 1.0) * ka_ref[:, sl])
            put_tiled(kd_ref, j, k_d)
            put_tiled(b_ref, j, kk * a)
            s_bonus = s_bonus + r * k_d * rk_ref[:, sl]
    inv = lax.rsqrt(jnp.maximum(_head_fold(s_kk, n_heads), 1e-24))
    s_bonus = _head_fold(s_bonus, n_heads)
    for c in range(n_keys):
        sl = _slab(c)
        nkk_ref[:, sl] = -nkk_ref[:, sl] * inv
        b0_ref[:, sl] = b0_ref[:, sl] * inv
        b1_ref[:, sl] = b1_ref[:, sl] * inv
    for j in range(nslab):
        bonus_ref[:, _slab(j)] = s_bonus * v_ref[:, _slab(j)]


def rwkv_elem(tt, dec2, iclr2, dec0, iclr0, r, k, v, kk, ka, rk, *, rd, ri, n_heads):
    t, d = r.shape
    bm = _pick(t, 32, 16)
    tw = tt.shape[1]
    dt = (d // n_heads) * LANES
    tile = pl.BlockSpec((bm, d), lambda i: (i, 0))
    tiled = pl.BlockSpec((bm, dt), lambda i: (i, 0))
    row = pl.BlockSpec((1, d), lambda i: (0, 0))
    return pl.pallas_call(
        functools.partial(_rwkv_elem_kernel, rd=rd, ri=ri, n_heads=n_heads),
        grid=(t // bm,),
        in_specs=[
            pl.BlockSpec((bm, tw), lambda i: (i, 0)),
            pl.BlockSpec((2, rd, d), lambda i: (0, 0, 0)),
            pl.BlockSpec((2, ri, d), lambda i: (0, 0, 0)),
            pl.BlockSpec((2, d), lambda i: (0, 0)),
            pl.BlockSpec((2, d), lambda i: (0, 0)),
            tile, tile, tile, row, row, row,
        ],
        out_specs=[tiled] * 8 + [tile],
        out_shape=[jax.ShapeDtypeStruct((t, dt), F32)] * 8 + [jax.ShapeDtypeStruct((t, d), F32)],
        compiler_params=_params("arbitrary"),
        name="rwkv_elem",
    )(tt, dec2, iclr2, dec0, iclr0, r, k, v, kk, ka, rk)


def _scan_kernel(rf, wf, kf, vf, af, bf, rb, wb, kb, vb, ab, bb, yf_ref, yb_ref,
                 sf_ref, sb_ref, *, tb, n):
    @pl.when(pl.program_id(0) == 0)
    def _():
        sf_ref[...] = jnp.zeros_like(sf_ref)
        sb_ref[...] = jnp.zeros_like(sb_ref)

    nvg = sf_ref.shape[1]
    rows = min(SUBLANES, nvg)
    nvb = nvg // rows

    def vslice(i):
        return pl.ds(i * rows, rows)

    def bcast(ref, t, kc):
        g, s = t
        return jnp.broadcast_to(ref[g, pl.ds(s, 1), pl.ds(kc * LANES, LANES)], (rows, LANES))

    def row(t):
        return t[0] * SUBLANES + t[1]

    def zeros():
        return [jnp.zeros((rows, LANES), F32) for _ in range(nvb)]

    def state_dot_a(t, a_ref, s_ref):
        sa = zeros()
        for kc in range(n):
            a_k = bcast(a_ref, t, kc)
            for i in range(nvb):
                sa[i] = sa[i] + s_ref[kc, vslice(i), :] * a_k
        return tuple(sa)

    def one_direction(t, t_next, r_ref, w_ref, k_ref, v_ref, a_ref, b_ref, y_ref, s_ref, sa):
        vv = [v_ref[row(t), vslice(i), :] for i in range(nvb)]
        y, sa_next = zeros(), zeros()
        for kc in range(n):
            w_k, b_k, k_k = bcast(w_ref, t, kc), bcast(b_ref, t, kc), bcast(k_ref, t, kc)
            r_k, a_k = bcast(r_ref, t, kc), bcast(a_ref, t_next, kc)
            for i in range(nvb):
                s_new = s_ref[kc, vslice(i), :] * w_k + sa[i] * b_k + vv[i] * k_k
                s_ref[kc, vslice(i), :] = s_new
                y[i] = y[i] + s_new * r_k
                sa_next[i] = sa_next[i] + s_new * a_k
        for i in range(nvb):
            y_ref[row(t), vslice(i), :] = y[i]
        return tuple(sa_next)

    ngroups = tb // SUBLANES

    def group(g, carry):
        sa_f, sa_b = carry
        g_next = jnp.minimum(g + 1, ngroups - 1)
        gb = ngroups - 1 - g
        gb_next = jnp.maximum(gb - 1, 0)
        for s in range(SUBLANES):
            t_next = (g, s + 1) if s + 1 < SUBLANES else (g_next, 0)
            sa_f = one_direction((g, s), t_next, rf, wf, kf, vf, af, bf, yf_ref, sf_ref, sa_f)
            u = SUBLANES - 1 - s
            u_next = (gb, u - 1) if u > 0 else (gb_next, SUBLANES - 1)
            sa_b = one_direction((gb, u), u_next, rb, wb, kb, vb, ab, bb, yb_ref, sb_ref, sa_b)
        return sa_f, sa_b

    init = (state_dot_a((0, 0), af, sf_ref),
            state_dot_a((ngroups - 1, SUBLANES - 1), ab, sb_ref))
    lax.fori_loop(0, ngroups, group, init)


def rwkv_scan(r, w0, w1, k0, k1, v, a, b0, b1, *, n_lat):
    t, d = v.shape
    n = r.shape[1] // LANES
    nvg = d // LANES
    tb = _pick(math.gcd(n_lat, t - n_lat), 32, 1)
    nb = t // tb
    first = n_lat // tb
    fwd = lambda i: (i + first) % nb
    bwd = lambda i: nb - 1 - i
    kview = lambda x: x.reshape(t // SUBLANES, SUBLANES, n * LANES)
    vview = lambda x: x.reshape(t, nvg, LANES)

    def specs(blk):
        ks = pl.BlockSpec((tb // SUBLANES, SUBLANES, n * LANES), lambda i: (blk(i), 0, 0))
        vs = pl.BlockSpec((tb, nvg, LANES), lambda i: (blk(i), 0, 0))
        return [ks, ks, ks, vs, ks, ks], vs

    in_f, out_f = specs(fwd)
    in_b, out_b = specs(bwd)
    y_f, y_b = pl.pallas_call(
        functools.partial(_scan_kernel, tb=tb, n=n),
        grid=(nb,),
        in_specs=in_f + in_b,
        out_specs=[out_f, out_b],
        out_shape=[jax.ShapeDtypeStruct((t, nvg, LANES), F32)] * 2,
        scratch_shapes=[pltpu.VMEM((n, nvg, LANES), F32)] * 2,
        compiler_params=_params("arbitrary"),
        name="rwkv_scan",
    )(kview(r), kview(w0), kview(k0), vview(v), kview(a), kview(b0),
      kview(r), kview(w1), kview(k1), vview(v), kview(a), kview(b1))
    return y_f.reshape(t, d), y_b.reshape(t, d)


def _rwkv_post_kernel(yf_ref, yb_ref, bonus_ref, g_ref, lw_ref, lb_ref, o_ref, *, n_heads):
    bm, d = yf_ref.shape
    nslab = d // LANES
    inv_n = 1.0 / (d // n_heads)
    s1 = jnp.zeros((bm, LANES), F32)
    for j in range(nslab):
        s1 = s1 + (yf_ref[:, _slab(j)] + yb_ref[:, _slab(j)])
    mu = _head_fold(s1, n_heads) * inv_n
    s2 = jnp.zeros((bm, LANES), F32)
    for j in range(nslab):
        yc = yf_ref[:, _slab(j)] + yb_ref[:, _slab(j)] - mu
        s2 = s2 + yc * yc
    rstd = lax.rsqrt(_head_fold(s2, n_heads) * inv_n + LNX_EPS)
    for j in range(nslab):
        sl = _slab(j)
        yn = (yf_ref[:, sl] + yb_ref[:, sl] - mu) * rstd
        out = yn * lw_ref[:, sl] + lb_ref[:, sl] + bonus_ref[:, sl]
        o_ref[:, sl] = (out * g_ref[:, sl].astype(F32)).astype(o_ref.dtype)


def rwkv_post(yf, yb, bonus, g, lnx_w, lnx_b, *, n_heads):
    t, d = yf.shape
    bm = _pick(t, 128, 16)
    tile = pl.BlockSpec((bm, d), lambda i: (i, 0))
    row = pl.BlockSpec((1, d), lambda i: (0, 0))
    return pl.pallas_call(
        functools.partial(_rwkv_post_kernel, n_heads=n_heads),
        grid=(t // bm,),
        in_specs=[tile, tile, tile, tile, row, row],
        out_specs=tile,
        out_shape=jax.ShapeDtypeStruct((t, d), BF16),
        compiler_params=_params("arbitrary"),
        name="rwkv_post",
    )(yf, yb, bonus, g, lnx_w.reshape(1, d), lnx_b.reshape(1, d))


def _qk_epi(scale):
    def epi(acc, ex, i_m):
        gain_ref, cos_ref, sin_ref = ex
        cos, sin = cos_ref[...], sin_ref[...]
        gain = gain_ref[...] * scale
        lane = lax.broadcasted_iota(jnp.int32, cos.shape, 1)
        even = (lane % 2) == 0
        outs = []
        for j in range(acc.shape[1] // HEAD_DIM):
            q = acc[:, j * HEAD_DIM:(j + 1) * HEAD_DIM]
            ms = jnp.mean(q * q, axis=-1, keepdims=True)
            q = q * lax.rsqrt(ms + NORM_EPS) * gain
            partner = jnp.where(even, pltpu.roll(q, HEAD_DIM - 1, 1), pltpu.roll(q, 1, 1))
            outs.append(q * cos + partner * sin)
        return jnp.concatenate(outs, axis=1)
    return epi


def qk_matmul(x, w, gain, cos_t, sin_t, *, rows, scale, lead=(), name="qk_mm"):
    n = w.shape[-1]
    bm = _pick(rows, 1024, 16)
    bn = _pick(n, 512, LANES)
    extras = [
        (gain.reshape(1, HEAD_DIM), (1, HEAD_DIM), lambda j, i, k: (0, 0)),
        (cos_t, (bm, HEAD_DIM), lambda j, i, k: (i, 0)),
        (sin_t, (bm, HEAD_DIM), lambda j, i, k: (i, 0)),
    ]
    return matmul(x, w, rows=rows, lead=lead, out_dtype=BF16, epi=_qk_epi(scale),
                  extras=extras, bm=bm, bn=bn, name=name)


def _attn_kernel(q_ref, k_ref, v_ref, o_ref, *, groups, bk, nchunks):
    bq = q_ref.shape[0]
    q = jnp.concatenate([q_ref[:, g * HEAD_DIM:(g + 1) * HEAD_DIM] for g in range(groups)], axis=0)
    rows = groups * bq

    def body(c, carry):
        m, l, acc = carry
        start = pl.multiple_of(c * bk, bk)
        kc = k_ref[pl.ds(start, bk), :]
        vc = v_ref[pl.ds(start, bk), :]
        s = lax.dot_general(q, kc, (((1,), (1,)), ((), ())), preferred_element_type=F32)
        m_new = jnp.maximum(m, jnp.max(s, axis=1, keepdims=True))
        alpha = jnp.exp2(m - m_new)
        p = jnp.exp2(s - m_new)
        l = alpha * l + jnp.sum(p, axis=1, keepdims=True)
        acc = alpha * acc + _bdot(p.astype(BF16), vc)
        return m_new, l, acc

    init = (jnp.full((rows, 1), -jnp.inf, F32), jnp.zeros((rows, 1), F32),
            jnp.zeros((rows, HEAD_DIM), F32))
    _, l, acc = lax.fori_loop(0, nchunks, body, init)
    out = acc / l
    for g in range(groups):
        o_ref[:, g * HEAD_DIM:(g + 1) * HEAD_DIM] = out[g * bq:(g + 1) * bq].astype(o_ref.dtype)


def attention(q, k, v, *, groups):
    s, dq = q.shape
    tk, dkv = k.shape
    n_kv = dkv // HEAD_DIM
    bq = _pick(s, 256, 16)
    bk = _pick(tk, 2816, 256) if tk % 256 == 0 else tk
    return pl.pallas_call(
        functools.partial(_attn_kernel, groups=groups, bk=bk, nchunks=tk // bk),
        grid=(n_kv, s // bq),
        in_specs=[
            pl.BlockSpec((bq, groups * HEAD_DIM), lambda h, i: (i, h)),
            pl.BlockSpec((tk, HEAD_DIM), lambda h, i: (0, h)),
            pl.BlockSpec((tk, HEAD_DIM), lambda h, i: (0, h)),
        ],
        out_specs=pl.BlockSpec((bq, groups * HEAD_DIM), lambda h, i: (i, h)),
        out_shape=jax.ShapeDtypeStruct((s, dq), BF16),
        compiler_params=_params("arbitrary", "arbitrary"),
        name="attention",
    )(q, k, v)


ROUTE_BLOCK = 256


def _route_plan(idx, wgt, n_experts):
    s = idx.shape[0]
    bm = ROUTE_BLOCK
    p_rows = 2 * s + n_experts * bm
    e_flat = idx.reshape(-1)
    onehot = (e_flat[:, None] == jnp.arange(n_experts, dtype=jnp.int32)[None, :]).astype(jnp.int32)
    counts = jnp.sum(onehot, axis=0)
    padded = ((counts + bm - 1) // bm) * bm
    ends = jnp.cumsum(padded)
    offs = ends - padded
    rank = jnp.cumsum(onehot, axis=0) - onehot
    pos = jnp.sum((offs[None, :] + rank) * onehot, axis=1)
    row_token = jnp.zeros((p_rows,), jnp.int32).at[pos].set(jnp.arange(2 * s, dtype=jnp.int32) // 2)
    row_weight = jnp.zeros((p_rows,), F32).at[pos].set(wgt.reshape(-1))
    blk_start = jnp.arange(p_rows // bm, dtype=jnp.int32) * bm
    block_expert = jnp.minimum(jnp.searchsorted(ends, blk_start, side="right"),
                               n_experts - 1).astype(jnp.int32)
    return pos.reshape(s, 2).astype(jnp.int32), row_token, row_weight.reshape(p_rows, 1), block_expert


def _row_copy(src_hbm, row, dst_vmem, slot, sem):
    return pltpu.make_async_copy(src_hbm.at[row], dst_vmem.at[slot], sem)


def _gather_rows_kernel(tok_ref, x_hbm, o_ref, sem, *, rows_per_step):
    base = pl.program_id(0) * rows_per_step

    def start(i, c):
        _row_copy(x_hbm, tok_ref[base + i], o_ref, i, sem).start()
        return c

    def wait(i, c):
        _row_copy(x_hbm, 0, o_ref, i, sem).wait()
        return c

    lax.fori_loop(0, rows_per_step, start, 0)
    lax.fori_loop(0, rows_per_step, wait, 0)


def gather_rows(x, row_token):
    s, d = x.shape
    p_rows = row_token.shape[0]
    rows_per_step = ROUTE_BLOCK
    x3 = x.reshape(s, d // LANES, LANES)
    grid_spec = pltpu.PrefetchScalarGridSpec(
        num_scalar_prefetch=1,
        grid=(p_rows // rows_per_step,),
        in_specs=[pl.BlockSpec(memory_space=pl.ANY)],
        out_specs=pl.BlockSpec((rows_per_step, d // LANES, LANES), lambda i, tok: (i, 0, 0)),
        scratch_shapes=[pltpu.SemaphoreType.DMA(())],
    )
    out = pl.pallas_call(
        functools.partial(_gather_rows_kernel, rows_per_step=rows_per_step),
        grid_spec=grid_spec,
        out_shape=jax.ShapeDtypeStruct((p_rows, d // LANES, LANES), x.dtype),
        compiler_params=_params("arbitrary"),
        name="moe_gather",
    )(row_token, x3)
    return out.reshape(p_rows, d)


def _combine_kernel(pos_ref, y_hbm, h_ref, g_ref, fn_ref, o_ref, buf, sem, *, rows_per_step):
    base = pl.program_id(0) * rows_per_step

    def copy(row, slot, i):
        return pltpu.make_async_copy(y_hbm.at[pl.ds(row, 1), :], buf.at[slot, pl.ds(i, 1), :], sem)

    def start(i, c):
        for slot in range(TOP_K):
            copy(pos_ref[(base + i) * TOP_K + slot], slot, i).start()
        return c

    def wait(i, c):
        for slot in range(TOP_K):
            copy(0, slot, i).wait()
        return c

    lax.fori_loop(0, rows_per_step, start, 0)
    lax.fori_loop(0, rows_per_step, wait, 0)
    x = h_ref[...] + g_ref[...] * (buf[0] + buf[1])
    ms = jnp.mean(x * x, axis=-1, keepdims=True)
    o_ref[...] = x * lax.rsqrt(ms + NORM_EPS) * fn_ref[...]


def combine_rows(y, pos, h, gate, final_gain):
    s, d = h.shape
    rows_per_step = _pick(s, 128)
    blk = pl.BlockSpec((rows_per_step, d), lambda i, pos: (i, 0))
    vec = pl.BlockSpec((1, d), lambda i, pos: (0, 0))
    grid_spec = pltpu.PrefetchScalarGridSpec(
        num_scalar_prefetch=1,
        grid=(s // rows_per_step,),
        in_specs=[pl.BlockSpec(memory_space=pl.ANY), blk, vec, vec],
        out_specs=blk,
        scratch_shapes=[pltpu.VMEM((TOP_K, rows_per_step, d), F32),
                        pltpu.SemaphoreType.DMA(())],
    )
    return pl.pallas_call(
        functools.partial(_combine_kernel, rows_per_step=rows_per_step),
        grid_spec=grid_spec,
        out_shape=jax.ShapeDtypeStruct((s, d), F32),
        compiler_params=_params("arbitrary"),
        name="moe_combine",
    )(pos.reshape(-1), y, h, gate.reshape(1, d), final_gain.reshape(1, d))


def _head_minor_cols(w, n_heads):
    lead = w.shape[:-1]
    n = w.shape[-1] // n_heads
    return jnp.swapaxes(w.reshape(lead + (n_heads, n)), -1, -2).reshape(lead + (n_heads * n,))


def _head_minor_rows(w, n_heads):
    rows, cols = w.shape
    return jnp.swapaxes(w.reshape(n_heads, rows // n_heads, cols), 0, 1).reshape(rows, cols)


def _pad_cols(w, mult):
    pad = (-w.shape[-1]) % mult
    return jnp.pad(w, [(0, 0)] * (w.ndim - 1) + [(0, pad)]) if pad else w


def kernel(x, c, ctx, c_ctx, ada_w, ada_b, norm_gains, rwkv_mix, rwkv_wr, rwkv_wk, rwkv_wv, rwkv_wo, rwkv_dec0, rwkv_dec1, rwkv_dec2, rwkv_iclr0, rwkv_iclr1, rwkv_iclr2, rwkv_gate1, rwkv_gate2, rwkv_kk, rwkv_ka, rwkv_rk, rwkv_lnx_w, rwkv_lnx_b, ffn_w_gate, ffn_w_up, ffn_w_down, attn_wq, attn_wk, attn_wv, attn_wo, attn_q_gain, attn_k_gain, moe_router, moe_w_gate, moe_w_up, moe_w_down, final_norm):
    batch, n_lat, d = x.shape
    n_ctx = ctx.shape[1]
    depth = ada_w.shape[0]
    assert batch == 1 and depth == 2 and rwkv_dec0.shape[1] == 2
    t = n_lat + n_ctx
    n_experts = moe_router.shape[-1]
    assert n_experts <= LANES and TOP_K == 2

    h = jnp.concatenate([x[0], ctx[0]], axis=0)

    cond8 = jnp.zeros((SUBLANES, d), F32).at[0].set(c[0]).at[1].set(c_ctx)
    mod = ada_params(cond8, ada_w, ada_b)[:, :2, :].reshape(depth, 2, N_MOD, d)

    xr, xw, xk, xv, xa, xg = rwkv_prep(h, norm_gains[0, 0], mod[0], rwkv_mix[0], n_lat=n_lat)
    nh = d // RWKV_HEAD
    hm = functools.partial(_head_minor_cols, n_heads=nh)
    r = matmul(xr, hm(rwkv_wr[0]).astype(BF16), rows=t, name="rwkv_r")
    k = matmul(xk, hm(rwkv_wk[0]).astype(BF16), rows=t, name="rwkv_k")
    v = matmul(xv, hm(rwkv_wv[0]).astype(BF16), rows=t, name="rwkv_v")

    rd, ri = rwkv_dec1.shape[-1], rwkv_iclr1.shape[-1]
    dec1 = jnp.concatenate([rwkv_dec1[0, 0], rwkv_dec1[0, 1]], axis=1)
    iclr1 = jnp.concatenate([rwkv_iclr1[0, 0], rwkv_iclr1[0, 1]], axis=1)
    gate1 = _pad_cols(rwkv_gate1[0], LANES)
    rg = gate1.shape[1]
    pieces = [rwkv_dec2[0].reshape(2 * rd, d), rwkv_iclr2[0].reshape(2 * ri, d), rwkv_gate2[0],
              rwkv_dec0[0], rwkv_iclr0[0], rwkv_kk[0:1], rwkv_ka[0:1], rwkv_rk[0].reshape(1, d),
              rwkv_lnx_w[0:1], rwkv_lnx_b[0:1]]
    stacked = hm(jnp.concatenate(pieces, axis=0))
    bounds = np.cumsum([0] + [p.shape[0] for p in pieces])
    (dec2, iclr2, gate2, dec0, iclr0, kk_w, ka_w, rk_w, lnx_w, lnx_b) = [
        stacked[bounds[i]:bounds[i + 1]] for i in range(len(pieces))]
    gate2 = jnp.pad(gate2, ((0, rg - gate2.shape[0]), (0, 0)))
    tw = matmul(xw, dec1, rows=t, out_dtype=BF16, epi=lambda a, e, i: jnp.tanh(a), name="lora_w")
    ta = matmul(xa, iclr1, rows=t, out_dtype=BF16, name="lora_a")
    tg = matmul(xg, gate1, rows=t, out_dtype=BF16, epi=lambda a, e, i: _sigmoid(a), name="lora_g")
    g = matmul(tg, gate2, rows=t, out_dtype=BF16, name="rwkv_gate")

    tt = jnp.concatenate([tw, ta], axis=1)
    w0, w1, k0, k1, b0, b1, nkk, rt, bonus = rwkv_elem(
        tt, dec2.reshape(2, rd, d).astype(BF16), iclr2.reshape(2, ri, d).astype(BF16), dec0, iclr0,
        r, k, v, kk_w, ka_w, rk_w, rd=rd, ri=ri, n_heads=nh)

    y_f, y_b = rwkv_scan(rt, w0, w1, k0, k1, v, nkk, b0, b1, n_lat=n_lat)
    yo = rwkv_post(y_f, y_b, bonus, g, lnx_w, lnx_b, n_heads=nh)
    h = residual_matmul(yo, _head_minor_rows(rwkv_wo[0], nh).astype(BF16), h, mod[0], rows=t,
                        n_lat=n_lat, gate_idx=2, name="rwkv_out")

    nf = norm_mod(h, norm_gains[0, 1], mod[0], rows=t, n_lat=n_lat, sub=1)
    hid = glu(nf, ffn_w_gate, ffn_w_up, rows=t)
    h = residual_matmul(hid, ffn_w_down.astype(BF16), h, mod[0], rows=t, n_lat=n_lat, gate_idx=5,
                        lead=(0,), bk=_pick(ffn_w_down.shape[1], 5504, LANES), name="ffn_down")

    na = norm_mod(h, norm_gains[1, 0], mod[1], rows=t, n_lat=n_lat, sub=0)
    pairs = HEAD_DIM // 4
    inv_freq = ROPE_THETA ** (-jnp.arange(pairs, dtype=F32) / pairs)
    pos = jnp.arange(n_lat)
    ang = jnp.concatenate([(pos // GRID_W).astype(F32)[:, None] * inv_freq,
                           (pos % GRID_W).astype(F32)[:, None] * inv_freq], axis=-1)
    sign = jnp.tile(jnp.asarray([-1.0, 1.0], F32), HEAD_DIM // 2)
    cos_t = jnp.concatenate([jnp.repeat(jnp.cos(ang), 2, axis=-1),
                             jnp.ones((n_ctx, HEAD_DIM), F32)], axis=0)
    sin_t = jnp.concatenate([jnp.repeat(jnp.sin(ang), 2, axis=-1) * sign,
                             jnp.zeros((n_ctx, HEAD_DIM), F32)], axis=0)
    q = qk_matmul(na, attn_wq, attn_q_gain[0], cos_t, sin_t, rows=n_lat,
                  scale=HEAD_DIM ** -0.5 * math.log2(math.e), lead=(0,), name="attn_q")
    kx = qk_matmul(na, attn_wk, attn_k_gain[0], cos_t, sin_t, rows=t, scale=1.0, lead=(0,),
                   name="attn_k")
    vx = matmul(na, attn_wv, rows=t, lead=(0,), out_dtype=BF16, name="attn_v")
    groups = attn_wq.shape[-1] // attn_wk.shape[-1]
    o = attention(q, kx, vx, groups=groups)
    h = residual_matmul(o, attn_wo, h, mod[1], rows=n_lat, n_lat=n_lat, gate_idx=2, lead=(0,),
                        name="attn_out")

    nm, ridx, rwgt = norm_router(h, norm_gains[1, 1], mod[1, 0], _pad_cols(moe_router[0], LANES),
                                 rows=n_lat, n_experts=n_experts)
    pos, row_token, row_weight, block_expert = _route_plan(ridx[:, :TOP_K], rwgt[:, :TOP_K],
                                                           n_experts)
    p_rows = row_token.shape[0]
    xs = gather_rows(nm, row_token)
    hid = glu(xs, moe_w_gate[0], moe_w_up[0], rows=p_rows, bm=ROUTE_BLOCK, bn=512,
              block_expert=block_expert, row_scale=row_weight)
    ys = matmul(hid, moe_w_down[0], rows=p_rows, block_group=block_expert, bm=ROUTE_BLOCK,
                name="moe_down")
    return combine_rows(ys, pos, h[:n_lat], mod[1, 0, 5], final_norm)[None]
```

```python
import functools
import math

import numpy as np
import jax
import jax.numpy as jnp
from jax import lax
from jax.experimental import pallas as pl
from jax.experimental.pallas import tpu as pltpu

F32 = jnp.float32
BF16 = jnp.bfloat16

N_MOD = 6
NORM_EPS = 1e-6
LNX_EPS = 64e-5
RWKV_HEAD = 64
HEAD_DIM = 128
GRID_W = 64
ROPE_THETA = 10000.0
TOP_K = 2
LANES = 128
SUBLANES = 8
VMEM_LIMIT = 56 * 1024 * 1024


def _pick(n, target, mult=SUBLANES):
    best = None
    for b in range(mult, min(n, target) + 1, mult):
        if n % b == 0:
            best = b
    return best if best is not None else n


def _params(*sem):
    return pltpu.CompilerParams(dimension_semantics=sem, vmem_limit_bytes=VMEM_LIMIT)


def _sigmoid(x):
    return 1.0 / (1.0 + jnp.exp(-x))


def _silu(x):
    return x * _sigmoid(x)


def _bdot(a, b):
    return jnp.dot(a, b, preferred_element_type=F32)


def _ada_kernel(c_ref, w_ref, b_ref, o_ref):
    s = _silu(c_ref[...]).astype(BF16)
    o_ref[...] = _bdot(s, w_ref[...].astype(BF16)) + b_ref[...]


def ada_params(cond8, ada_w, ada_b):
    depth, d, n = ada_w.shape
    bn = _pick(n, 512, LANES)
    return pl.pallas_call(
        _ada_kernel,
        grid=(depth, n // bn),
        in_specs=[
            pl.BlockSpec((SUBLANES, d), lambda l, j: (0, 0)),
            pl.BlockSpec((None, d, bn), lambda l, j: (l, 0, j)),
            pl.BlockSpec((None, 1, bn), lambda l, j: (l, 0, j)),
        ],
        out_specs=pl.BlockSpec((None, SUBLANES, bn), lambda l, j: (l, 0, j)),
        out_shape=jax.ShapeDtypeStruct((depth, SUBLANES, n), F32),
        compiler_params=_params("arbitrary", "arbitrary"),
        name="ada_params",
    )(cond8, ada_w, ada_b.reshape(depth, 1, n))


def _norm_mod(x, gain, shift, scale):
    ms = jnp.mean(x * x, axis=-1, keepdims=True)
    return (x * lax.rsqrt(ms + NORM_EPS) * gain) * (1.0 + scale) + shift


def _norm_mod_kernel(h_ref, g_ref, m_ref, o_ref, *, sub):
    n = _norm_mod(h_ref[...], g_ref[...], m_ref[3 * sub:3 * sub + 1, :],
                  m_ref[3 * sub + 1:3 * sub + 2, :])
    o_ref[...] = n.astype(o_ref.dtype)


def norm_mod(h, gain, mod, *, rows, n_lat, sub):
    t, d = h.shape
    bm = _pick(math.gcd(n_lat, t - n_lat) if t > n_lat else n_lat, 256)
    nlat_blocks = n_lat // bm
    return pl.pallas_call(
        functools.partial(_norm_mod_kernel, sub=sub),
        grid=(rows // bm,),
        in_specs=[
            pl.BlockSpec((bm, d), lambda i: (i, 0)),
            pl.BlockSpec((1, d), lambda i: (0, 0)),
            pl.BlockSpec((None, N_MOD, d), lambda i: (jnp.where(i >= nlat_blocks, 1, 0), 0, 0)),
        ],
        out_specs=pl.BlockSpec((bm, d), lambda i: (i, 0)),
        out_shape=jax.ShapeDtypeStruct((rows, d), BF16),
        compiler_params=_params("arbitrary"),
        name="norm_mod",
    )(h, gain.reshape(1, d), mod)


def _norm_router_kernel(h_ref, g_ref, m_ref, r_ref, o_ref, idx_ref, wgt_ref, *, n_experts):
    n = _norm_mod(h_ref[...], g_ref[...], m_ref[3:4, :], m_ref[4:5, :])
    o_ref[...] = n.astype(o_ref.dtype)
    logits = jnp.dot(n, r_ref[...], precision=lax.Precision.HIGHEST,
                     preferred_element_type=F32)
    lane = lax.broadcasted_iota(jnp.int32, logits.shape, 1)
    neg = jnp.float32(-jnp.inf)
    lg = jnp.where(lane < n_experts, logits, neg)
    m1 = jnp.max(lg, axis=-1, keepdims=True)
    i1 = jnp.min(jnp.where(lg == m1, lane, LANES), axis=-1, keepdims=True)
    lg2 = jnp.where(lane == i1, neg, lg)
    m2 = jnp.max(lg2, axis=-1, keepdims=True)
    i2 = jnp.min(jnp.where(lg2 == m2, lane, LANES), axis=-1, keepdims=True)
    e = jnp.exp(m2 - m1)
    w1 = 1.0 / (1.0 + e)
    idx_ref[...] = jnp.where(lane == 0, i1, jnp.where(lane == 1, i2, 0))
    wgt_ref[...] = jnp.where(lane == 0, w1, jnp.where(lane == 1, 1.0 - w1, 0.0))


def norm_router(h, gain, mod_lat, router_pad, *, rows, n_experts):
    t, d = h.shape
    bm = _pick(rows, 256)
    lane_blk = pl.BlockSpec((bm, LANES), lambda i: (i, 0))
    return pl.pallas_call(
        functools.partial(_norm_router_kernel, n_experts=n_experts),
        grid=(rows // bm,),
        in_specs=[
            pl.BlockSpec((bm, d), lambda i: (i, 0)),
            pl.BlockSpec((1, d), lambda i: (0, 0)),
            pl.BlockSpec((N_MOD, d), lambda i: (0, 0)),
            pl.BlockSpec((d, LANES), lambda i: (0, 0)),
        ],
        out_specs=[pl.BlockSpec((bm, d), lambda i: (i, 0)), lane_blk, lane_blk],
        out_shape=[jax.ShapeDtypeStruct((rows, d), BF16),
                   jax.ShapeDtypeStruct((rows, LANES), jnp.int32),
                   jax.ShapeDtypeStruct((rows, LANES), F32)],
        compiler_params=_params("arbitrary"),
        name="norm_router",
    )(h, gain.reshape(1, d), mod_lat, router_pad)


def _rwkv_prep_kernel(h_ref, hp_ref, hn_ref, g_ref, m_ref, mix_ref, *o_refs,
                      nlat_blocks, nblocks):
    i = pl.program_id(0)
    gain, shift, scale = g_ref[...], m_ref[0:1, :], m_ref[1:2, :]
    n = _norm_mod(h_ref[...], gain, shift, scale)
    n_prev = _norm_mod(hp_ref[...], gain, shift, scale)[SUBLANES - 1:SUBLANES, :]
    n_next = _norm_mod(hn_ref[...], gain, shift, scale)[0:1, :]
    first = jnp.logical_or(i == 0, i == nlat_blocks)
    last = jnp.logical_or(i == nlat_blocks - 1, i == nblocks - 1)
    n_prev = jnp.where(first, 0.0, n_prev)
    n_next = jnp.where(last, 0.0, n_next)
    bm = n.shape[0]
    row = lax.broadcasted_iota(jnp.int32, n.shape, 0)
    prev = jnp.where(row == 0, n_prev, pltpu.roll(n, 1, 0))
    nxt = jnp.where(row == bm - 1, n_next, pltpu.roll(n, bm - 1, 0))
    dh = 0.5 * (prev + nxt) - n
    for j, o_ref in enumerate(o_refs):
        o_ref[...] = (n + dh * mix_ref[j:j + 1, :]).astype(o_ref.dtype)


def rwkv_prep(h, gain, mod, mix, *, n_lat):
    t, d = h.shape
    bm = _pick(math.gcd(n_lat, t - n_lat), 128)
    nblocks, nlat_blocks = t // bm, n_lat // bm
    r8 = bm // SUBLANES
    n8 = t // SUBLANES
    return pl.pallas_call(
        functools.partial(_rwkv_prep_kernel, nlat_blocks=nlat_blocks, nblocks=nblocks),
        grid=(nblocks,),
        in_specs=[
            pl.BlockSpec((bm, d), lambda i: (i, 0)),
            pl.BlockSpec((SUBLANES, d), lambda i: (jnp.maximum(i * r8 - 1, 0), 0)),
            pl.BlockSpec((SUBLANES, d), lambda i: (jnp.minimum((i + 1) * r8, n8 - 1), 0)),
            pl.BlockSpec((1, d), lambda i: (0, 0)),
            pl.BlockSpec((None, N_MOD, d), lambda i: (jnp.where(i >= nlat_blocks, 1, 0), 0, 0)),
            pl.BlockSpec((6, d), lambda i: (0, 0)),
        ],
        out_specs=[pl.BlockSpec((bm, d), lambda i: (i, 0))] * 6,
        out_shape=[jax.ShapeDtypeStruct((t, d), BF16)] * 6,
        compiler_params=_params("arbitrary"),
        name="rwkv_prep",
    )(h, h, h, gain.reshape(1, d), mod, mix)


def _mm_kernel(*refs, nk, n_extra, epi, grouped):
    if grouped:
        refs = refs[1:]
    x_ref, w_ref = refs[0], refs[1]
    extras = refs[2:2 + n_extra]
    o_ref = refs[2 + n_extra]
    i_m = pl.program_id(1)
    part = _bdot(x_ref[...].astype(BF16), w_ref[...].astype(BF16))
    if nk == 1:
        o_ref[...] = epi(part, extras, i_m).astype(o_ref.dtype)
    else:
        acc_ref = refs[3 + n_extra]
        k = pl.program_id(2)

        @pl.when(k == 0)
        def _():
            acc_ref[...] = part

        @pl.when(k > 0)
        def _():
            acc_ref[...] += part

        @pl.when(k == nk - 1)
        def _():
            o_ref[...] = epi(acc_ref[...], extras, i_m).astype(o_ref.dtype)


def matmul(x, w, *, rows, lead=(), block_group=None, out_dtype=F32, epi=None, extras=(),
           bm=1056, bn=512, bk=4096, name="matmul"):
    kdim = x.shape[1]
    n = w.shape[-1]
    bm = _pick(rows, bm, 16)
    bn = _pick(n, bn, LANES)
    bk = _pick(kdim, bk, LANES)
    nk = kdim // bk
    if epi is None:
        epi = lambda acc, ex, i_m: acc
    grouped = block_group is not None
    if grouped:
        w_spec = pl.BlockSpec((None, bk, bn), lambda j, i, k, grp: (grp[i], k, j))
    else:
        w_spec = pl.BlockSpec((None,) * len(lead) + (bk, bn),
                              lambda j, i, k: tuple(lead) + (k, j))
    in_specs = [pl.BlockSpec((bm, bk), lambda j, i, k, *_: (i, k)), w_spec]
    args = [x, w]
    for arr, bshape, imap in extras:
        in_specs.append(pl.BlockSpec(bshape, imap))
        args.append(arr)
    if grouped:
        args = [block_group] + args
    grid_spec = pltpu.PrefetchScalarGridSpec(
        num_scalar_prefetch=1 if grouped else 0,
        grid=(n // bn, rows // bm, nk),
        in_specs=in_specs,
        out_specs=pl.BlockSpec((bm, bn), lambda j, i, k, *_: (i, j)),
        scratch_shapes=[pltpu.VMEM((bm, bn), F32)] if nk > 1 else [],
    )
    return pl.pallas_call(
        functools.partial(_mm_kernel, nk=nk, n_extra=len(extras), epi=epi, grouped=grouped),
        grid_spec=grid_spec,
        out_shape=jax.ShapeDtypeStruct((rows, n), out_dtype),
        compiler_params=_params("arbitrary", "arbitrary", "arbitrary"),
        name=name,
    )(*args)


def _residual_epi(n_lat, gate_idx):
    def epi(acc, ex, i_m):
        h_ref, m_ref = ex
        bm = acc.shape[0]
        row = i_m * bm + lax.broadcasted_iota(jnp.int32, (bm, 1), 0)
        gate = jnp.where(row < n_lat, m_ref[0, gate_idx:gate_idx + 1, :],
                         m_ref[1, gate_idx:gate_idx + 1, :])
        return h_ref[...] + gate * acc
    return epi


def residual_matmul(x, w, h, mod, *, rows, n_lat, gate_idx, lead=(), bk=4096, name="residual_mm"):
    n = w.shape[-1]
    bm = _pick(rows, 1056, 16)
    bn = _pick(n, 512, LANES)
    extras = [
        (h, (bm, bn), lambda j, i, k: (i, j)),
        (mod, (2, N_MOD, bn), lambda j, i, k: (0, 0, j)),
    ]
    return matmul(x, w, rows=rows, lead=lead, epi=_residual_epi(n_lat, gate_idx),
                  extras=extras, bm=bm, bn=bn, bk=bk, name=name)


def _glu_kernel(*refs, grouped):
    if grouped:
        _, x_ref, wg_ref, wu_ref, scale_ref, o_ref = refs
    else:
        x_ref, wg_ref, wu_ref, o_ref = refs
    x = x_ref[...]
    g = _bdot(x, wg_ref[...].astype(BF16))
    u = _bdot(x, wu_ref[...].astype(BF16))
    hdn = _silu(g) * u
    if grouped:
        hdn = hdn * scale_ref[...]
    o_ref[...] = hdn.astype(o_ref.dtype)


def glu(x, wg, wu, *, rows, bm=1056, bn=256, block_expert=None, row_scale=None):
    n_exp, kdim, f = wg.shape
    grouped = block_expert is not None
    assert grouped or n_exp == 1
    bm = _pick(rows, bm, 16)
    bn = _pick(f, bn, LANES)
    w_map = (lambda j, i, be: (be[i], 0, j)) if grouped else (lambda j, i: (0, 0, j))
    in_specs = [
        pl.BlockSpec((bm, kdim), lambda j, i, *_: (i, 0)),
        pl.BlockSpec((None, kdim, bn), w_map),
        pl.BlockSpec((None, kdim, bn), w_map),
    ]
    args = [x, wg, wu]
    if grouped:
        in_specs.append(pl.BlockSpec((bm, 1), lambda j, i, *_: (i, 0)))
        args = [block_expert] + args + [row_scale]
    grid_spec = pltpu.PrefetchScalarGridSpec(
        num_scalar_prefetch=1 if grouped else 0,
        grid=(f // bn, rows // bm),
        in_specs=in_specs,
        out_specs=pl.BlockSpec((bm, bn), lambda j, i, *_: (i, j)),
    )
    return pl.pallas_call(
        functools.partial(_glu_kernel, grouped=grouped),
        grid_spec=grid_spec,
        out_shape=jax.ShapeDtypeStruct((rows, f), BF16),
        compiler_params=_params("arbitrary", "arbitrary"),
        name="glu",
    )(*args)


def _head_fold(x, n_heads):
    shift = n_heads
    while shift < LANES:
        x = x + pltpu.roll(x, shift, 1)
        shift *= 2
    return x


def _slab(j):
    return slice(j * LANES, (j + 1) * LANES)


def _rwkv_elem_kernel(t_ref, dec2_ref, iclr2_ref, dec0_ref, iclr0_ref, r_ref, k_ref, v_ref,
                      kk_ref, ka_ref, rk_ref,
                      w0_ref, w1_ref, k0_ref, k1_ref, b0_ref, b1_ref, nkk_ref, rt_ref, bonus_ref,
                      *, rd, ri, n_heads):
    bm, d = r_ref.shape
    nslab = d // LANES
    rep = LANES // n_heads
    n_keys = d // n_heads

    def put_tiled(ref, j, x):
        for i in range(rep):
            piece = x[:, i * n_heads:(i + 1) * n_heads]
            ref[:, _slab(j * rep + i)] = jnp.concatenate([piece] * rep, axis=1)

    tt = t_ref[...]
    tw = [tt[:, i * rd:(i + 1) * rd] for i in range(2)]
    ta = [tt[:, 2 * rd + i * ri:2 * rd + (i + 1) * ri] for i in range(2)]
    outs = ((w0_ref, k0_ref, b0_ref), (w1_ref, k1_ref, b1_ref))
    s_kk = jnp.zeros((bm, LANES), F32)
    s_bonus = jnp.zeros((bm, LANES), F32)
    for j in range(nslab):
        sl = _slab(j)
        k, r = k_ref[:, sl], r_ref[:, sl]
        put_tiled(rt_ref, j, r)
        kk = k * kk_ref[:, sl]
        put_tiled(nkk_ref, j, kk)
        s_kk = s_kk + kk * kk
        for dd, (w_ref, kd_ref, b_ref) in enumerate(outs):
            z = dec0_ref[dd:dd + 1, sl] + _bdot(tw[dd], dec2_ref[dd, :, sl])
            w_log = -(jnp.maximum(-z, 0.0) + jnp.log(1.0 + jnp.exp(-jnp.abs(z)))) - 0.5
            put_tiled(w_ref, j, jnp.exp(-jnp.exp(w_log)))
            a = _sigmoid(iclr0_ref[dd:dd + 1, sl] + _bdot(ta[dd], iclr2_ref[dd, :, sl]))
            k_d = k * (1.0 + (a - 1.0) * ka_ref[:, sl])
            put_tiled(kd_ref, j, k_d)
            put_tiled(b_ref, j, kk * a)
            s_bonus = s_bonus + r * k_d * rk_ref[:, sl]
    inv = lax.rsqrt(jnp.maximum(_head_fold(s_kk, n_heads), 1e-24))
    s_bonus = _head_fold(s_bonus, n_heads)
    for c in range(n_keys):
        sl = _slab(c)
        nkk_ref[:, sl] = -nkk_ref[:, sl] * inv
        b0_ref[:, sl] = b0_ref[:, sl] * inv
        b1_ref[:, sl] = b1_ref[:, sl] * inv
    for j in range(nslab):
        bonus_ref[:, _slab(j)] = s_bonus * v_ref[:, _slab(j)]


def rwkv_elem(tt, dec2, iclr2, dec0, iclr0, r, k, v, kk, ka, rk, *, rd, ri, n_heads):
    t, d = r.shape
    bm = _pick(t, 32, 16)
    tw = tt.shape[1]
    dt = (d // n_heads) * LANES
    tile = pl.BlockSpec((bm, d), lambda i: (i, 0))
    tiled = pl.BlockSpec((bm, dt), lambda i: (i, 0))
    row = pl.BlockSpec((1, d), lambda i: (0, 0))
    return pl.pallas_call(
        functools.partial(_rwkv_elem_kernel, rd=rd, ri=ri, n_heads=n_heads),
        grid=(t // bm,),
        in_specs=[
            pl.BlockSpec((bm, tw), lambda i: (i, 0)),
            pl.BlockSpec((2, rd, d), lambda i: (0, 0, 0)),
            pl.BlockSpec((2, ri, d), lambda i: (0, 0, 0)),
            pl.BlockSpec((2, d), lambda i: (0, 0)),
            pl.BlockSpec((2, d), lambda i: (0, 0)),
            tile, tile, tile, row, row, row,
        ],
        out_specs=[tiled] * 8 + [tile],
        out_shape=[jax.ShapeDtypeStruct((t, dt), F32)] * 8 + [jax.ShapeDtypeStruct((t, d), F32)],
        compiler_params=_params("arbitrary"),
        name="rwkv_elem",
    )(tt, dec2, iclr2, dec0, iclr0, r, k, v, kk, ka, rk)


def _scan_kernel(rf, wf, kf, vf, af, bf, rb, wb, kb, vb, ab, bb, yf_ref, yb_ref,
                 sf_ref, sb_ref, *, tb, n):
    @pl.when(pl.program_id(0) == 0)
    def _():
        sf_ref[...] = jnp.zeros_like(sf_ref)
        sb_ref[...] = jnp.zeros_like(sb_ref)

    nvg = sf_ref.shape[1]
    rows = min(SUBLANES, nvg)
    nvb = nvg // rows

    def vslice(i):
        return pl.ds(i * rows, rows)

    def bcast(ref, t, kc):
        g, s = t
        return jnp.broadcast_to(ref[g, pl.ds(s, 1), pl.ds(kc * LANES, LANES)], (rows, LANES))

    def row(t):
        return t[0] * SUBLANES + t[1]

    def zeros():
        return [jnp.zeros((rows, LANES), F32) for _ in range(nvb)]

    def state_dot_a(t, a_ref, s_ref):
        sa = zeros()
        for kc in range(n):
            a_k = bcast(a_ref, t, kc)
            for i in range(nvb):
                sa[i] = sa[i] + s_ref[kc, vslice(i), :] * a_k
        return tuple(sa)

    def one_direction(t, t_next, r_ref, w_ref, k_ref, v_ref, a_ref, b_ref, y_ref, s_ref, sa):
        vv = [v_ref[row(t), vslice(i), :] for i in range(nvb)]
        y, sa_next = zeros(), zeros()
        for kc in range(n):
            w_k, b_k, k_k = bcast(w_ref, t, kc), bcast(b_ref, t, kc), bcast(k_ref, t, kc)
            r_k, a_k = bcast(r_ref, t, kc), bcast(a_ref, t_next, kc)
            for i in range(nvb):
                s_new = s_ref[kc, vslice(i), :] * w_k + sa[i] * b_k + vv[i] * k_k
                s_ref[kc, vslice(i), :] = s_new
                y[i] = y[i] + s_new * r_k
                sa_next[i] = sa_next[i] + s_new * a_k
        for i in range(nvb):
            y_ref[row(t), vslice(i), :] = y[i]
        return tuple(sa_next)

    ngroups = tb // SUBLANES

    def group(g, carry):
        sa_f, sa_b = carry
        g_next = jnp.minimum(g + 1, ngroups - 1)
        gb = ngroups - 1 - g
        gb_next = jnp.maximum(gb - 1, 0)
        for s in range(SUBLANES):
            t_next = (g, s + 1) if s + 1 < SUBLANES else (g_next, 0)
            sa_f = one_direction((g, s), t_next, rf, wf, kf, vf, af, bf, yf_ref, sf_ref, sa_f)
            u = SUBLANES - 1 - s
            u_next = (gb, u - 1) if u > 0 else (gb_next, SUBLANES - 1)
            sa_b = one_direction((gb, u), u_next, rb, wb, kb, vb, ab, bb, yb_ref, sb_ref, sa_b)
        return sa_f, sa_b

    init = (state_dot_a((0, 0), af, sf_ref),
            state_dot_a((ngroups - 1, SUBLANES - 1), ab, sb_ref))
    lax.fori_loop(0, ngroups, group, init)


def rwkv_scan(r, w0, w1, k0, k1, v, a, b0, b1, *, n_lat):
    t, d = v.shape
    n = r.shape[1] // LANES
    nvg = d // LANES
    tb = _pick(math.gcd(n_lat, t - n_lat), 32, 1)
    nb = t // tb
    first = n_lat // tb
    fwd = lambda i: (i + first) % nb
    bwd = lambda i: nb - 1 - i
    kview = lambda x: x.reshape(t // SUBLANES, SUBLANES, n * LANES)
    vview = lambda x: x.reshape(t, nvg, LANES)

    def specs(blk):
        ks = pl.BlockSpec((tb // SUBLANES, SUBLANES, n * LANES), lambda i: (blk(i), 0, 0))
        vs = pl.BlockSpec((tb, nvg, LANES), lambda i: (blk(i), 0, 0))
        return [ks, ks, ks, vs, ks, ks], vs

    in_f, out_f = specs(fwd)
    in_b, out_b = specs(bwd)
    y_f, y_b = pl.pallas_call(
        functools.partial(_scan_kernel, tb=tb, n=n),
        grid=(nb,),
        in_specs=in_f + in_b,
        out_specs=[out_f, out_b],
        out_shape=[jax.ShapeDtypeStruct((t, nvg, LANES), F32)] * 2,
        scratch_shapes=[pltpu.VMEM((n, nvg, LANES), F32)] * 2,
        compiler_params=_params("arbitrary"),
        name="rwkv_scan",
    )(kview(r), kview(w0), kview(k0), vview(v), kview(a), kview(b0),
      kview(r), kview(w1), kview(k1), vview(v), kview(a), kview(b1))
    return y_f.reshape(t, d), y_b.reshape(t, d)


def _rwkv_post_kernel(yf_ref, yb_ref, bonus_ref, g_ref, lw_ref, lb_ref, o_ref, *, n_heads):
    bm, d = yf_ref.shape
    nslab = d // LANES
    inv_n = 1.0 / (d // n_heads)
    s1 = jnp.zeros((bm, LANES), F32)
    for j in range(nslab):
        s1 = s1 + (yf_ref[:, _slab(j)] + yb_ref[:, _slab(j)])
    mu = _head_fold(s1, n_heads) * inv_n
    s2 = jnp.zeros((bm, LANES), F32)
    for j in range(nslab):
        yc = yf_ref[:, _slab(j)] + yb_ref[:, _slab(j)] - mu
        s2 = s2 + yc * yc
    rstd = lax.rsqrt(_head_fold(s2, n_heads) * inv_n + LNX_EPS)
    for j in range(nslab):
        sl = _slab(j)
        yn = (yf_ref[:, sl] + yb_ref[:, sl] - mu) * rstd
        out = yn * lw_ref[:, sl] + lb_ref[:, sl] + bonus_ref[:, sl]
        o_ref[:, sl] = (out * g_ref[:, sl].astype(F32)).astype(o_ref.dtype)


def rwkv_post(yf, yb, bonus, g, lnx_w, lnx_b, *, n_heads):
    t, d = yf.shape
    bm = _pick(t, 128, 16)
    tile = pl.BlockSpec((bm, d), lambda i: (i, 0))
    row = pl.BlockSpec((1, d), lambda i: (0, 0))
    return pl.pallas_call(
        functools.partial(_rwkv_post_kernel, n_heads=n_heads),
        grid=(t // bm,),
        in_specs=[tile, tile, tile, tile, row, row],
        out_specs=tile,
        out_shape=jax.ShapeDtypeStruct((t, d), BF16),
        compiler_params=_params("arbitrary"),
        name="rwkv_post",
    )(yf, yb, bonus, g, lnx_w.reshape(1, d), lnx_b.reshape(1, d))


def _qk_epi(scale):
    def epi(acc, ex, i_m):
        gain_ref, cos_ref, sin_ref = ex
        cos, sin = cos_ref[...], sin_ref[...]
        gain = gain_ref[...] * scale
        lane = lax.broadcasted_iota(jnp.int32, cos.shape, 1)
        even = (lane % 2) == 0
        outs = []
        for j in range(acc.shape[1] // HEAD_DIM):
            q = acc[:, j * HEAD_DIM:(j + 1) * HEAD_DIM]
            ms = jnp.mean(q * q, axis=-1, keepdims=True)
            q = q * lax.rsqrt(ms + NORM_EPS) * gain
            partner = jnp.where(even, pltpu.roll(q, HEAD_DIM - 1, 1), pltpu.roll(q, 1, 1))
            outs.append(q * cos + partner * sin)
        return jnp.concatenate(outs, axis=1)
    return epi


def qk_matmul(x, w, gain, cos_t, sin_t, *, rows, scale, lead=(), name="qk_mm"):
    n = w.shape[-1]
    bm = _pick(rows, 1024, 16)
    bn = _pick(n, 512, LANES)
    extras = [
        (gain.reshape(1, HEAD_DIM), (1, HEAD_DIM), lambda j, i, k: (0, 0)),
        (cos_t, (bm, HEAD_DIM), lambda j, i, k: (i, 0)),
        (sin_t, (bm, HEAD_DIM), lambda j, i, k: (i, 0)),
    ]
    return matmul(x, w, rows=rows, lead=lead, out_dtype=BF16, epi=_qk_epi(scale),
                  extras=extras, bm=bm, bn=bn, name=name)


def _attn_kernel(q_ref, k_ref, v_ref, o_ref, *, groups, bk, nchunks):
    bq = q_ref.shape[0]
    q = jnp.concatenate([q_ref[:, g * HEAD_DIM:(g + 1) * HEAD_DIM] for g in range(groups)], axis=0)
    rows = groups * bq

    def body(c, carry):
        m, l, acc = carry
        start = pl.multiple_of(c * bk, bk)
        kc = k_ref[pl.ds(start, bk), :]
        vc = v_ref[pl.ds(start, bk), :]
        s = lax.dot_general(q, kc, (((1,), (1,)), ((), ())), preferred_element_type=F32)
        m_new = jnp.maximum(m, jnp.max(s, axis=1, keepdims=True))
        alpha = jnp.exp2(m - m_new)
        p = jnp.exp2(s - m_new)
        l = alpha * l + jnp.sum(p, axis=1, keepdims=True)
        acc = alpha * acc + _bdot(p.astype(BF16), vc)
        return m_new, l, acc

    init = (jnp.full((rows, 1), -jnp.inf, F32), jnp.zeros((rows, 1), F32),
            jnp.zeros((rows, HEAD_DIM), F32))
    _, l, acc = lax.fori_loop(0, nchunks, body, init)
    out = acc / l
    for g in range(groups):
        o_ref[:, g * HEAD_DIM:(g + 1) * HEAD_DIM] = out[g * bq:(g + 1) * bq].astype(o_ref.dtype)


def attention(q, k, v, *, groups):
    s, dq = q.shape
    tk, dkv = k.shape
    n_kv = dkv // HEAD_DIM
    bq = _pick(s, 256, 16)
    bk = _pick(tk, 2816, 256) if tk % 256 == 0 else tk
    return pl.pallas_call(
        functools.partial(_attn_kernel, groups=groups, bk=bk, nchunks=tk // bk),
        grid=(n_kv, s // bq),
        in_specs=[
            pl.BlockSpec((bq, groups * HEAD_DIM), lambda h, i: (i, h)),
            pl.BlockSpec((tk, HEAD_DIM), lambda h, i: (0, h)),
            pl.BlockSpec((tk, HEAD_DIM), lambda h, i: (0, h)),
        ],
        out_specs=pl.BlockSpec((bq, groups * HEAD_DIM), lambda h, i: (i, h)),
        out_shape=jax.ShapeDtypeStruct((s, dq), BF16),
        compiler_params=_params("arbitrary", "arbitrary"),
        name="attention",
    )(q, k, v)


ROUTE_BLOCK = 256


def _route_plan(idx, wgt, n_experts):
    s = idx.shape[0]
    bm = ROUTE_BLOCK
    p_rows = 2 * s + n_experts * bm
    e_flat = idx.reshape(-1)
    onehot = (e_flat[:, None] == jnp.arange(n_experts, dtype=jnp.int32)[None, :]).astype(jnp.int32)
    counts = jnp.sum(onehot, axis=0)
    padded = ((counts + bm - 1) // bm) * bm
    ends = jnp.cumsum(padded)
    offs = ends - padded
    rank = jnp.cumsum(onehot, axis=0) - onehot
    pos = jnp.sum((offs[None, :] + rank) * onehot, axis=1)
    row_token = jnp.zeros((p_rows,), jnp.int32).at[pos].set(jnp.arange(2 * s, dtype=jnp.int32) // 2)
    row_weight = jnp.zeros((p_rows,), F32).at[pos].set(wgt.reshape(-1))
    blk_start = jnp.arange(p_rows // bm, dtype=jnp.int32) * bm
    block_expert = jnp.minimum(jnp.searchsorted(ends, blk_start, side="right"),
                               n_experts - 1).astype(jnp.int32)
    return pos.reshape(s, 2).astype(jnp.int32), row_token, row_weight.reshape(p_rows, 1), block_expert


def _row_copy(src_hbm, row, dst_vmem, slot, sem):
    return pltpu.make_async_copy(src_hbm.at[row], dst_vmem.at[slot], sem)


def _gather_rows_kernel(tok_ref, x_hbm, o_ref, sem, *, rows_per_step):
    base = pl.program_id(0) * rows_per_step

    def start(i, c):
        for par in range(2):
            r = 2 * i + par
            _row_copy(x_hbm, tok_ref[base + r], o_ref, r, sem).start(priority=par)
        return c

    def wait(i, c):
        _row_copy(x_hbm, 0, o_ref, i, sem).wait()
        return c

    lax.fori_loop(0, rows_per_step // 2, start, 0)
    lax.fori_loop(0, rows_per_step, wait, 0)


def gather_rows(x, row_token):
    s, d = x.shape
    p_rows = row_token.shape[0]
    rows_per_step = ROUTE_BLOCK
    x3 = x.reshape(s, d // LANES, LANES)
    grid_spec = pltpu.PrefetchScalarGridSpec(
        num_scalar_prefetch=1,
        grid=(p_rows // rows_per_step,),
        in_specs=[pl.BlockSpec(memory_space=pl.ANY)],
        out_specs=pl.BlockSpec((rows_per_step, d // LANES, LANES), lambda i, tok: (i, 0, 0)),
        scratch_shapes=[pltpu.SemaphoreType.DMA(())],
    )
    out = pl.pallas_call(
        functools.partial(_gather_rows_kernel, rows_per_step=rows_per_step),
        grid_spec=grid_spec,
        out_shape=jax.ShapeDtypeStruct((p_rows, d // LANES, LANES), x.dtype),
        compiler_params=_params("arbitrary"),
        name="moe_gather",
    )(row_token, x3)
    return out.reshape(p_rows, d)


def _combine_kernel(pos_ref, y_hbm, h_ref, g_ref, fn_ref, o_ref, buf, sem, *, rows_per_step):
    base = pl.program_id(0) * rows_per_step

    def copy(row, slot, i):
        return pltpu.make_async_copy(y_hbm.at[pl.ds(row, 1), :], buf.at[slot, pl.ds(i, 1), :], sem)

    def start(i, c):
        for slot in range(TOP_K):
            copy(pos_ref[(base + i) * TOP_K + slot], slot, i).start(priority=slot)
        return c

    def wait(i, c):
        for slot in range(TOP_K):
            copy(0, slot, i).wait()
        return c

    lax.fori_loop(0, rows_per_step, start, 0)
    lax.fori_loop(0, rows_per_step, wait, 0)
    x = h_ref[...] + g_ref[...] * (buf[0] + buf[1])
    ms = jnp.mean(x * x, axis=-1, keepdims=True)
    o_ref[...] = x * lax.rsqrt(ms + NORM_EPS) * fn_ref[...]


def combine_rows(y, pos, h, gate, final_gain):
    s, d = h.shape
    rows_per_step = _pick(s, 128)
    blk = pl.BlockSpec((rows_per_step, d), lambda i, pos: (i, 0))
    vec = pl.BlockSpec((1, d), lambda i, pos: (0, 0))
    grid_spec = pltpu.PrefetchScalarGridSpec(
        num_scalar_prefetch=1,
        grid=(s // rows_per_step,),
        in_specs=[pl.BlockSpec(memory_space=pl.ANY), blk, vec, vec],
        out_specs=blk,
        scratch_shapes=[pltpu.VMEM((TOP_K, rows_per_step, d), F32),
                        pltpu.SemaphoreType.DMA(())],
    )
    return pl.pallas_call(
        functools.partial(_combine_kernel, rows_per_step=rows_per_step),
        grid_spec=grid_spec,
        out_shape=jax.ShapeDtypeStruct((s, d), F32),
        compiler_params=_params("arbitrary"),
        name="moe_combine",
    )(pos.reshape(-1), y, h, gate.reshape(1, d), final_gain.reshape(1, d))


def _head_minor_cols(w, n_heads):
    lead = w.shape[:-1]
    n = w.shape[-1] // n_heads
    return jnp.swapaxes(w.reshape(lead + (n_heads, n)), -1, -2).reshape(lead + (n_heads * n,))


def _head_minor_rows(w, n_heads):
    rows, cols = w.shape
    return jnp.swapaxes(w.reshape(n_heads, rows // n_heads, cols), 0, 1).reshape(rows, cols)


def _pad_cols(w, mult):
    pad = (-w.shape[-1]) % mult
    return jnp.pad(w, [(0, 0)] * (w.ndim - 1) + [(0, pad)]) if pad else w


def kernel(x, c, ctx, c_ctx, ada_w, ada_b, norm_gains, rwkv_mix, rwkv_wr, rwkv_wk, rwkv_wv, rwkv_wo, rwkv_dec0, rwkv_dec1, rwkv_dec2, rwkv_iclr0, rwkv_iclr1, rwkv_iclr2, rwkv_gate1, rwkv_gate2, rwkv_kk, rwkv_ka, rwkv_rk, rwkv_lnx_w, rwkv_lnx_b, ffn_w_gate, ffn_w_up, ffn_w_down, attn_wq, attn_wk, attn_wv, attn_wo, attn_q_gain, attn_k_gain, moe_router, moe_w_gate, moe_w_up, moe_w_down, final_norm):
    batch, n_lat, d = x.shape
    n_ctx = ctx.shape[1]
    depth = ada_w.shape[0]
    assert batch == 1 and depth == 2 and rwkv_dec0.shape[1] == 2
    t = n_lat + n_ctx
    n_experts = moe_router.shape[-1]
    assert n_experts <= LANES and TOP_K == 2

    h = jnp.concatenate([x[0], ctx[0]], axis=0)

    cond8 = jnp.zeros((SUBLANES, d), F32).at[0].set(c[0]).at[1].set(c_ctx)
    mod = ada_params(cond8, ada_w, ada_b)[:, :2, :].reshape(depth, 2, N_MOD, d)

    xr, xw, xk, xv, xa, xg = rwkv_prep(h, norm_gains[0, 0], mod[0], rwkv_mix[0], n_lat=n_lat)
    nh = d // RWKV_HEAD
    hm = functools.partial(_head_minor_cols, n_heads=nh)
    r = matmul(xr, hm(rwkv_wr[0]).astype(BF16), rows=t, name="rwkv_r")
    k = matmul(xk, hm(rwkv_wk[0]).astype(BF16), rows=t, name="rwkv_k")
    v = matmul(xv, hm(rwkv_wv[0]).astype(BF16), rows=t, name="rwkv_v")

    rd, ri = rwkv_dec1.shape[-1], rwkv_iclr1.shape[-1]
    dec1 = jnp.concatenate([rwkv_dec1[0, 0], rwkv_dec1[0, 1]], axis=1)
    iclr1 = jnp.concatenate([rwkv_iclr1[0, 0], rwkv_iclr1[0, 1]], axis=1)
    gate1 = _pad_cols(rwkv_gate1[0], LANES)
    rg = gate1.shape[1]
    pieces = [rwkv_dec2[0].reshape(2 * rd, d), rwkv_iclr2[0].reshape(2 * ri, d), rwkv_gate2[0],
              rwkv_dec0[0], rwkv_iclr0[0], rwkv_kk[0:1], rwkv_ka[0:1], rwkv_rk[0].reshape(1, d),
              rwkv_lnx_w[0:1], rwkv_lnx_b[0:1]]
    stacked = hm(jnp.concatenate(pieces, axis=0))
    bounds = np.cumsum([0] + [p.shape[0] for p in pieces])
    (dec2, iclr2, gate2, dec0, iclr0, kk_w, ka_w, rk_w, lnx_w, lnx_b) = [
        stacked[bounds[i]:bounds[i + 1]] for i in range(len(pieces))]
    gate2 = jnp.pad(gate2, ((0, rg - gate2.shape[0]), (0, 0)))
    tw = matmul(xw, dec1, rows=t, out_dtype=BF16, epi=lambda a, e, i: jnp.tanh(a), name="lora_w")
    ta = matmul(xa, iclr1, rows=t, out_dtype=BF16, name="lora_a")
    tg = matmul(xg, gate1, rows=t, out_dtype=BF16, epi=lambda a, e, i: _sigmoid(a), name="lora_g")
    g = matmul(tg, gate2, rows=t, out_dtype=BF16, name="rwkv_gate")

    tt = jnp.concatenate([tw, ta], axis=1)
    w0, w1, k0, k1, b0, b1, nkk, rt, bonus = rwkv_elem(
        tt, dec2.reshape(2, rd, d).astype(BF16), iclr2.reshape(2, ri, d).astype(BF16), dec0, iclr0,
        r, k, v, kk_w, ka_w, rk_w, rd=rd, ri=ri, n_heads=nh)

    y_f, y_b = rwkv_scan(rt, w0, w1, k0, k1, v, nkk, b0, b1, n_lat=n_lat)
    yo = rwkv_post(y_f, y_b, bonus, g, lnx_w, lnx_b, n_heads=nh)
    h = residual_matmul(yo, _head_minor_rows(rwkv_wo[0], nh).astype(BF16), h, mod[0], rows=t,
                        n_lat=n_lat, gate_idx=2, name="rwkv_out")

    nf = norm_mod(h, norm_gains[0, 1], mod[0], rows=t, n_lat=n_lat, sub=1)
    hid = glu(nf, ffn_w_gate, ffn_w_up, rows=t)
    h = residual_matmul(hid, ffn_w_down.astype(BF16), h, mod[0], rows=t, n_lat=n_lat, gate_idx=5,
                        lead=(0,), bk=_pick(ffn_w_down.shape[1], 5504, LANES), name="ffn_down")

    na = norm_mod(h, norm_gains[1, 0], mod[1], rows=t, n_lat=n_lat, sub=0)
    pairs = HEAD_DIM // 4
    inv_freq = ROPE_THETA ** (-jnp.arange(pairs, dtype=F32) / pairs)
    pos = jnp.arange(n_lat)
    ang = jnp.concatenate([(pos // GRID_W).astype(F32)[:, None] * inv_freq,
                           (pos % GRID_W).astype(F32)[:, None] * inv_freq], axis=-1)
    sign = jnp.tile(jnp.asarray([-1.0, 1.0], F32), HEAD_DIM // 2)
    cos_t = jnp.concatenate([jnp.repeat(jnp.cos(ang), 2, axis=-1),
                             jnp.ones((n_ctx, HEAD_DIM), F32)], axis=0)
    sin_t = jnp.concatenate([jnp.repeat(jnp.sin(ang), 2, axis=-1) * sign,
                             jnp.zeros((n_ctx, HEAD_DIM), F32)], axis=0)
    q = qk_matmul(na, attn_wq, attn_q_gain[0], cos_t, sin_t, rows=n_lat,
                  scale=HEAD_DIM ** -0.5 * math.log2(math.e), lead=(0,), name="attn_q")
    kx = qk_matmul(na, attn_wk, attn_k_gain[0], cos_t, sin_t, rows=t, scale=1.0, lead=(0,),
                   name="attn_k")
    vx = matmul(na, attn_wv, rows=t, lead=(0,), out_dtype=BF16, name="attn_v")
    groups = attn_wq.shape[-1] // attn_wk.shape[-1]
    o = attention(q, kx, vx, groups=groups)
    h = residual_matmul(o, attn_wo, h, mod[1], rows=n_lat, n_lat=n_lat, gate_idx=2, lead=(0,),
                        name="attn_out")

    nm, ridx, rwgt = norm_router(h, norm_gains[1, 1], mod[1, 0], _pad_cols(moe_router[0], LANES),
                                 rows=n_lat, n_experts=n_experts)
    pos, row_token, row_weight, block_expert = _route_plan(ridx[:, :TOP_K], rwgt[:, :TOP_K],
                                                           n_experts)
    p_rows = row_token.shape[0]
    xs = gather_rows(nm, row_token)
    hid = glu(xs, moe_w_gate[0], moe_w_up[0], rows=p_rows, bm=ROUTE_BLOCK, bn=512,
              block_expert=block_expert, row_scale=row_weight)
    ys = matmul(hid, moe_w_down[0], rows=p_rows, block_group=block_expert, bm=ROUTE_BLOCK,
                name="moe_down")
    return combine_rows(ys, pos, h[:n_lat], mod[1, 0, 5], final_norm)[None]
```
